```python
import math
import jax, jax.numpy as jnp
from jax import lax
import numpy as np

D_MODEL = 4096
BATCH = 1
SEQ = 8192
DEPTH = 2
DEC_BATCH = 16
DEC_SEQ = 32
PAST_LEN = 4096

CHUNK = 64
N_A_LAYERS = DEPTH // 2
N_B_LAYERS = DEPTH - N_A_LAYERS
RMS_EPS = 1e-6
SSM_GROUP = 16
SSM_GROUPS = D_MODEL // SSM_GROUP
SSM_STATE = 64
DT_MIN = 1e-3
DT_MAX = 1e-1
N_HEADS = 32
HEAD_DIM = D_MODEL // N_HEADS
PAST_CHUNKS = 8
BAND_PAST = PAST_CHUNKS * CHUNK
BAND = BAND_PAST + CHUNK
MAX_REL = 256
MEM_TOKENS = 256
MEM_HEADS = 4
MEM_HEAD_DIM = D_MODEL // MEM_HEADS
N_GROUPS = 8
EXPERTS_PER_GROUP = 8
N_EXPERTS = N_GROUPS * EXPERTS_PER_GROUP
TOP_K = 2
D_EXPERT = D_MODEL // 4
MOE_BLOCK = 128

kernel_name = 'yoco_s5_chunkband_hmoe_stream_step'


def rms_norm(x, g):
    xf = x.astype(jnp.float32)
    y = xf * lax.rsqrt(jnp.mean(xf * xf, axis=-1, keepdims=True) + RMS_EPS)
    return (y * g.astype(jnp.float32)).astype(x.dtype)


def s5_discretize(a_re, a_im, log_dt, b_re, b_im):
    f32 = jnp.float32
    a_re = a_re.astype(f32)
    a_im = a_im.astype(f32)
    dt = jnp.exp(log_dt.astype(f32))[:, None]
    mag = jnp.exp(a_re * dt)
    ab_re = mag * jnp.cos(a_im * dt)
    ab_im = mag * jnp.sin(a_im * dt)
    den = a_re * a_re + a_im * a_im
    zr = ab_re - 1.0
    f_re = (zr * a_re + ab_im * a_im) / den
    f_im = (ab_im * a_re - zr * a_im) / den
    b_re = b_re.astype(f32)
    b_im = b_im.astype(f32)
    bb_re = f_re[..., None] * b_re - f_im[..., None] * b_im
    bb_im = f_re[..., None] * b_im + f_im[..., None] * b_re
    return ab_re, ab_im, bb_re, bb_im


def _ssm_combine(e1, e2):
    a1r, a1i, b1r, b1i = e1
    a2r, a2i, b2r, b2i = e2
    return (a2r * a1r - a2i * a1i, a2r * a1i + a2i * a1r,
            a2r * b1r - a2i * b1i + b2r, a2r * b1i + a2i * b1r + b2i)


def s5_block(h_re, h_im, u, ab_re, ab_im, bb_re, bb_im, c_re, c_im):
    bu_re = jnp.einsum('blgp,gnp->blgn', u, bb_re)
    bu_im = jnp.einsum('blgp,gnp->blgn', u, bb_im)
    bu_re = bu_re.at[:, 0].add(ab_re * h_re - ab_im * h_im)
    bu_im = bu_im.at[:, 0].add(ab_re * h_im + ab_im * h_re)
    a_re = jnp.broadcast_to(ab_re, bu_re.shape)
    a_im = jnp.broadcast_to(ab_im, bu_im.shape)
    _, _, s_re, s_im = lax.associative_scan(_ssm_combine, (a_re, a_im, bu_re, bu_im), axis=1)
    y = jnp.einsum('blgn,gpn->blgp', s_re, c_re) - jnp.einsum('blgn,gpn->blgp', s_im, c_im)
    return y, s_re[:, -1], s_im[:, -1]


def s5_mixer(xn, h_re, h_im, a_re, a_im, log_dt, b_re, b_im, c_re, c_im, d_skip, w_glu, blocked):
    B, L, _ = xn.shape
    f32 = jnp.float32
    ab_re, ab_im, bb_re, bb_im = s5_discretize(a_re, a_im, log_dt, b_re, b_im)
    c_re = c_re.astype(f32)
    c_im = c_im.astype(f32)
    u = xn.astype(f32).reshape(B, L, SSM_GROUPS, SSM_GROUP)
    if blocked:
        nb = L // CHUNK
        ub = u.reshape(B, nb, CHUNK, SSM_GROUPS, SSM_GROUP).transpose(1, 0, 2, 3, 4)

        def step(carry, u_blk):
            y_blk, hr, hi = s5_block(carry[0], carry[1], u_blk, ab_re, ab_im, bb_re, bb_im, c_re, c_im)
            return (hr, hi), y_blk

        (h_re, h_im), yb = lax.scan(step, (h_re, h_im), ub)
        y = yb.transpose(1, 0, 2, 3, 4).reshape(B, L, D_MODEL)
    else:
        y, h_re, h_im = s5_block(h_re, h_im, u, ab_re, ab_im, bb_re, bb_im, c_re, c_im)
        y = y.reshape(B, L, D_MODEL)
    y = y + d_skip.astype(f32) * xn.astype(f32)
    g = jax.nn.gelu(y).astype(xn.dtype)
    val, gate = jnp.split(g @ w_glu, 2, axis=-1)
    return val * jax.nn.sigmoid(gate), h_re, h_im


def band_attention_prompt(q, k, v, table):
    B, S, H, Dh = q.shape
    nc = S // CHUNK
    pad = ((0, 0), (BAND_PAST, 0), (0, 0), (0, 0))
    kp = jnp.pad(k, pad)
    vp = jnp.pad(v, pad)
    rel = BAND_PAST + jnp.arange(CHUNK)[:, None] - jnp.arange(BAND)[None, :]
    bias = table[:, jnp.clip(rel, -MAX_REL, MAX_REL) + MAX_REL].astype(jnp.float32)
    scale = Dh ** -0.5
    qc = q.reshape(B, nc, CHUNK, H, Dh).transpose(1, 0, 2, 3, 4)

    def one_chunk(args):
        c, qb = args
        kb = lax.dynamic_slice_in_dim(kp, c * CHUNK, BAND, axis=1)
        vb = lax.dynamic_slice_in_dim(vp, c * CHUNK, BAND, axis=1)
        s = jnp.einsum('bqhd,bkhd->bhqk', qb, kb).astype(jnp.float32) * scale + bias
        kpos = (c - PAST_CHUNKS) * CHUNK + jnp.arange(BAND)
        s = jnp.where(kpos >= 0, s, -jnp.inf)
        p = jax.nn.softmax(s, axis=-1).astype(vb.dtype)
        return jnp.einsum('bhqk,bkhd->bqhd', p, vb)

    o = lax.map(one_chunk, (jnp.arange(nc), qc))
    return o.transpose(1, 0, 2, 3, 4).reshape(B, S, H, Dh)


def band_attention_sample(q, cache_k, cache_v, k_new, v_new, table):
    B, T, H, Dh = q.shape
    cb = cache_k.shape[1]
    k = jnp.concatenate([cache_k.astype(k_new.dtype), k_new], axis=1)
    v = jnp.concatenate([cache_v.astype(v_new.dtype), v_new], axis=1)
    rel = cb + jnp.arange(T)[:, None] - jnp.arange(cb + T)[None, :]
    bias = table[:, jnp.clip(rel, -MAX_REL, MAX_REL) + MAX_REL].astype(jnp.float32)
    s = jnp.einsum('bqhd,bkhd->bhqk', q, k).astype(jnp.float32) * (Dh ** -0.5) + bias
    p = jax.nn.softmax(s, axis=-1).astype(v.dtype)
    return jnp.einsum('bhqk,bkhd->bqhd', p, v)


def memory_kv(mem, g, w_kv):
    B, M, _ = mem.shape
    k, v = jnp.split(rms_norm(mem, g) @ w_kv, 2, axis=-1)
    return (k.reshape(B, M, MEM_HEADS, MEM_HEAD_DIM), v.reshape(B, M, MEM_HEADS, MEM_HEAD_DIM))


def memory_attend(xn, mk, mv, w_q, w_o):
    B, L, _ = xn.shape
    q = (xn @ w_q).reshape(B, L, MEM_HEADS, MEM_HEAD_DIM)
    s = jnp.einsum('blhd,bmhd->bhlm', q, mk.astype(q.dtype)).astype(jnp.float32) * (MEM_HEAD_DIM ** -0.5)
    p = jax.nn.softmax(s, axis=-1).astype(q.dtype)
    o = jnp.einsum('bhlm,bmhd->blhd', p, mv.astype(q.dtype)).reshape(B, L, D_MODEL)
    return o @ w_o


def grouped_expert_mlp(h, eidx, gate, w_in, w_out):
    T = h.shape[0]
    A = T * TOP_K
    blk = max(8, min(MOE_BLOCK, A // N_EXPERTS))
    n_blocks = -(-A // blk) + N_EXPERTS
    flat_e = eidx.reshape(A).astype(jnp.int32)
    flat_tok = jnp.repeat(jnp.arange(T, dtype=jnp.int32), TOP_K)
    order = jnp.argsort(flat_e)
    se = flat_e[order]
    stok = flat_tok[order]
    sgate = gate.reshape(A)[order]
    counts = jnp.zeros((N_EXPERTS,), jnp.int32).at[flat_e].add(1)
    padded = (counts + blk - 1) // blk * blk
    pad_end = jnp.cumsum(padded)
    pad_start = pad_end - padded
    start = jnp.cumsum(counts) - counts
    dest = pad_start[se] + jnp.arange(A, dtype=jnp.int32) - start[se]
    slot_tok = jnp.zeros((n_blocks * blk,), jnp.int32).at[dest].set(stok)
    block_e = jnp.minimum(jnp.searchsorted(pad_end, jnp.arange(n_blocks, dtype=jnp.int32) * blk, side='right'),
                          N_EXPERTS - 1)
    xs = h[slot_tok].reshape(n_blocks, blk, D_MODEL)

    def expert_block(args):
        xb, e = args
        g, u = jnp.split(xb @ w_in[e], 2, axis=-1)
        return (jax.nn.silu(g) * u) @ w_out[e]

    ys = lax.map(expert_block, (xs, block_e)).reshape(n_blocks * blk, D_MODEL)
    contrib = ys[dest] * sgate[:, None].astype(ys.dtype)
    return jnp.zeros_like(h).at[stok].add(contrib)


def moe_ffn(xn, w_group, b_group, w_expert, b_expert, w_in, w_out):
    shp = xn.shape
    h = xn.reshape(-1, D_MODEL)
    T = h.shape[0]
    f32 = jnp.float32
    lg = (h @ w_group).astype(f32) + b_group.astype(f32)
    _, g_idx = lax.top_k(lg, 1)
    p_group = jnp.take_along_axis(jax.nn.softmax(lg, axis=-1), g_idx, axis=-1)
    le = ((h @ w_expert).astype(f32) + b_expert.astype(f32)).reshape(T, N_GROUPS, EXPERTS_PER_GROUP)
    le = jnp.take_along_axis(le, g_idx[:, :, None], axis=1)[:, 0]
    top_v, top_j = lax.top_k(le, TOP_K)
    gate = p_group * jax.nn.softmax(top_v, axis=-1)
    eidx = g_idx * EXPERTS_PER_GROUP + top_j
    return grouped_expert_mlp(h, eidx, gate, w_in, w_out).reshape(shp)


def _trunk(x, prompt_mode, ssm_re, ssm_im, band_k, band_v, mem_k_in, mem_v_in, mem, w):
    B, L, _ = x.shape
    f32 = jnp.float32
    out_re, out_im, out_mk, out_mv = [], [], [], []
    k_sh = None
    v_sh = None
    for layer in range(DEPTH):
        xn = rms_norm(x, w['norm_mix'][layer])
        if layer < N_A_LAYERS:
            a = layer
            if prompt_mode:
                h_re = jnp.zeros((B, SSM_GROUPS, SSM_STATE), f32)
                h_im = jnp.zeros((B, SSM_GROUPS, SSM_STATE), f32)
            else:
                h_re = ssm_re[a].astype(f32)
                h_im = ssm_im[a].astype(f32)
            mix, h_re, h_im = s5_mixer(xn, h_re, h_im, w['ssm_a_re'][a], w['ssm_a_im'][a], w['ssm_log_dt'][a],
                                       w['ssm_b_re'][a], w['ssm_b_im'][a], w['ssm_c_re'][a], w['ssm_c_im'][a],
                                       w['ssm_d'][a], w['ssm_w_glu'][a], prompt_mode)
            out_re.append(h_re)
            out_im.append(h_im)
        else:
            if layer == N_A_LAYERS:
                k_flat, v_flat = jnp.split(rms_norm(x, w['norm_kv']) @ w['w_kv_shared'], 2, axis=-1)
                k_sh = k_flat.reshape(B, L, N_HEADS, HEAD_DIM)
                v_sh = v_flat.reshape(B, L, N_HEADS, HEAD_DIM)
            b = layer - N_A_LAYERS
            q = (xn @ w['attn_w_q'][b]).reshape(B, L, N_HEADS, HEAD_DIM)
            if prompt_mode:
                o = band_attention_prompt(q, k_sh, v_sh, w['attn_rel_bias'][b])
            else:
                o = band_attention_sample(q, band_k, band_v, k_sh, v_sh, w['attn_rel_bias'][b])
            mix = o.reshape(B, L, D_MODEL) @ w['attn_w_o'][b]
        x = x + mix
        if prompt_mode:
            mk, mv = memory_kv(mem, w['norm_memin'][layer], w['mem_w_kv'][layer])
            out_mk.append(mk)
            out_mv.append(mv)
        else:
            mk = mem_k_in[layer]
            mv = mem_v_in[layer]
        x = x + memory_attend(rms_norm(x, w['norm_mem'][layer]), mk, mv, w['mem_w_q'][layer], w['mem_w_o'][layer])
        x = x + moe_ffn(rms_norm(x, w['norm_ffn'][layer]), w['moe_w_group'][layer], w['moe_b_group'][layer],
                        w['moe_w_expert'][layer], w['moe_b_expert'][layer], w['moe_w_in'][layer],
                        w['moe_w_out'][layer])
    keep = min(BAND_PAST, L)
    y = rms_norm(x, w['norm_final'])
    return y, jnp.stack(out_re), jnp.stack(out_im), k_sh[:, L - keep:], v_sh[:, L - keep:], out_mk, out_mv


def setup_inputs(seed: int = 0) -> dict:
    key = jax.random.key(seed)
    keys = jax.random.split(key, 40)
    it = iter(range(40))
    f32 = jnp.float32
    D = D_MODEL

    def nrm(shape, scale):
        return scale * jax.random.normal(keys[next(it)], shape, f32)

    def gain(shape):
        return 1.0 + 0.01 * jax.random.normal(keys[next(it)], shape, f32)

    cb = min(BAND_PAST, PAST_LEN)
    n_idx = jnp.arange(SSM_STATE, dtype=f32)
    inp = {}
    inp['x_prompt'] = nrm((BATCH, SEQ, D), 1.0)
    inp['x_sample'] = nrm((DEC_BATCH, DEC_SEQ, D), 1.0)
    inp['state_ssm_re'] = nrm((N_A_LAYERS, DEC_BATCH, SSM_GROUPS, SSM_STATE), 0.1)
    inp['state_ssm_im'] = nrm((N_A_LAYERS, DEC_BATCH, SSM_GROUPS, SSM_STATE), 0.1)
    inp['cache_band_k'] = nrm((DEC_BATCH, cb, N_HEADS, HEAD_DIM), 1.0)
    inp['cache_band_v'] = nrm((DEC_BATCH, cb, N_HEADS, HEAD_DIM), 1.0)
    inp['cache_mem_k'] = nrm((DEPTH, DEC_BATCH, MEM_TOKENS, MEM_HEADS, MEM_HEAD_DIM), 1.0)
    inp['cache_mem_v'] = nrm((DEPTH, DEC_BATCH, MEM_TOKENS, MEM_HEADS, MEM_HEAD_DIM), 1.0)
    inp['mem_prompt'] = nrm((BATCH, MEM_TOKENS, D), 1.0)
    inp['norm_mix'] = gain((DEPTH, D))
    inp['norm_mem'] = gain((DEPTH, D))
    inp['norm_memin'] = gain((DEPTH, D))
    inp['norm_ffn'] = gain((DEPTH, D))
    inp['norm_kv'] = gain((D,))
    inp['norm_final'] = gain((D,))
    inp['ssm_a_re'] = -0.5 + nrm((N_A_LAYERS, SSM_GROUPS, SSM_STATE), 0.01)
    inp['ssm_a_im'] = math.pi * n_idx + nrm((N_A_LAYERS, SSM_GROUPS, SSM_STATE), 0.01)
    inp['ssm_log_dt'] = jax.random.uniform(keys[next(it)], (N_A_LAYERS, SSM_GROUPS), f32,
                                           math.log(DT_MIN), math.log(DT_MAX))
    inp['ssm_b_re'] = nrm((N_A_LAYERS, SSM_GROUPS, SSM_STATE, SSM_GROUP), (2.0 * SSM_GROUP) ** -0.5)
    inp['ssm_b_im'] = nrm((N_A_LAYERS, SSM_GROUPS, SSM_STATE, SSM_GROUP), (2.0 * SSM_GROUP) ** -0.5)
    inp['ssm_c_re'] = nrm((N_A_LAYERS, SSM_GROUPS, SSM_GROUP, SSM_STATE), (2.0 * SSM_STATE) ** -0.5)
    inp['ssm_c_im'] = nrm((N_A_LAYERS, SSM_GROUPS, SSM_GROUP, SSM_STATE), (2.0 * SSM_STATE) ** -0.5)
    inp['ssm_d'] = nrm((N_A_LAYERS, D), 1.0)
    inp['ssm_w_glu'] = nrm((N_A_LAYERS, D, 2 * D), D ** -0.5)
    inp['w_kv_shared'] = nrm((D, 2 * D), D ** -0.5)
    inp['attn_w_q'] = nrm((N_B_LAYERS, D, D), D ** -0.5)
    inp['attn_rel_bias'] = nrm((N_B_LAYERS, N_HEADS, 2 * MAX_REL + 1), 0.1)
    inp['attn_w_o'] = nrm((N_B_LAYERS, D, D), D ** -0.5)
    inp['mem_w_q'] = nrm((DEPTH, D, D), D ** -0.5)
    inp['mem_w_kv'] = nrm((DEPTH, D, 2 * D), D ** -0.5)
    inp['mem_w_o'] = nrm((DEPTH, D, D), D ** -0.5)
    inp['moe_w_group'] = nrm((DEPTH, D, N_GROUPS), D ** -0.5)
    inp['moe_b_group'] = nrm((DEPTH, N_GROUPS), 0.01)
    inp['moe_w_expert'] = nrm((DEPTH, D, N_EXPERTS), D ** -0.5)
    inp['moe_b_expert'] = nrm((DEPTH, N_EXPERTS), 0.01)
    inp['moe_w_in'] = nrm((DEPTH, N_EXPERTS, D, 2 * D_EXPERT), D ** -0.5)
    inp['moe_w_out'] = nrm((DEPTH, N_EXPERTS, D_EXPERT, D), D_EXPERT ** -0.5)
    return inp


def reference(x_prompt, x_sample, state_ssm_re, state_ssm_im, cache_band_k, cache_band_v, cache_mem_k,
              cache_mem_v, mem_prompt, norm_mix, norm_mem, norm_memin, norm_ffn, norm_kv, norm_final,
              ssm_a_re, ssm_a_im, ssm_log_dt, ssm_b_re, ssm_b_im, ssm_c_re, ssm_c_im, ssm_d, ssm_w_glu,
              w_kv_shared, attn_w_q, attn_rel_bias, attn_w_o, mem_w_q, mem_w_kv, mem_w_o,
              moe_w_group, moe_b_group, moe_w_expert, moe_b_expert, moe_w_in, moe_w_out):
    w = dict(norm_mix=norm_mix, norm_mem=norm_mem, norm_memin=norm_memin, norm_ffn=norm_ffn,
             norm_kv=norm_kv, norm_final=norm_final, ssm_a_re=ssm_a_re, ssm_a_im=ssm_a_im,
             ssm_log_dt=ssm_log_dt, ssm_b_re=ssm_b_re, ssm_b_im=ssm_b_im, ssm_c_re=ssm_c_re,
             ssm_c_im=ssm_c_im, ssm_d=ssm_d, ssm_w_glu=ssm_w_glu, w_kv_shared=w_kv_shared,
             attn_w_q=attn_w_q, attn_rel_bias=attn_rel_bias, attn_w_o=attn_w_o, mem_w_q=mem_w_q,
             mem_w_kv=mem_w_kv, mem_w_o=mem_w_o, moe_w_group=moe_w_group, moe_b_group=moe_b_group,
             moe_w_expert=moe_w_expert, moe_b_expert=moe_b_expert, moe_w_in=moe_w_in, moe_w_out=moe_w_out)
    y_prompt, re_p, im_p, bk_p, bv_p, mk_p, mv_p = _trunk(x_prompt, True, None, None, None, None, None, None,
                                                         mem_prompt, w)
    y_sample, re_s, im_s, bk_s, bv_s, _, _ = _trunk(x_sample, False, state_ssm_re, state_ssm_im, cache_band_k,
                                                   cache_band_v, cache_mem_k, cache_mem_v, None, w)
    mem_k_prompt = jnp.stack(mk_p)
    mem_v_prompt = jnp.stack(mv_p)
    return (y_prompt, y_sample, re_p, im_p, re_s, im_s, bk_p, bv_p, bk_s, bv_s, mem_k_prompt, mem_v_prompt)
```

```python
import functools
import math

import jax
import jax.numpy as jnp
from jax import lax
from jax.experimental import pallas as pl
from jax.experimental.pallas import tpu as pltpu

F32 = jnp.float32
BF16 = jnp.bfloat16

CHUNK = 64
RMS_EPS = 1e-6
SSM_GROUP = 16
SSM_STATE = 64
N_HEADS = 32
PAST_CHUNKS = 8
MAX_REL = 256
MEM_HEADS = 4
N_GROUPS = 8
EXPERTS_PER_GROUP = 8
N_EXPERTS = N_GROUPS * EXPERTS_PER_GROUP
TOP_K = 2

V7X_VMEM_BYTES = 64 * 1024 * 1024
LANES = 128
MXU_DIM = 256

ROW_TILE = 512
NORM_TILE = 256
COL_TILE = 512
SSM_SUB = MXU_DIM // SSM_GROUP
SSM_GROUPS_PER_STEP = 8
BAND_Q_ROWS = 1024
BAND_WIN = (PAST_CHUNKS + 2) * CHUNK
SAMPLE_HEADS_PER_STEP = 8
MOE_ROWS = 384
MOE_SUB = 128
MOE_COL = 256
NEG = -1e30


def _params(sem, vmem_bytes):
    limit = min(int(vmem_bytes) + (4 << 20), V7X_VMEM_BYTES - (8 << 20))
    return pltpu.CompilerParams(dimension_semantics=sem, vmem_limit_bytes=limit)


def _rms_body(n_out, x_ref, *refs):
    g_refs, o_refs = refs[:n_out], refs[n_out:]
    x = x_ref[...]
    y = x * lax.rsqrt(jnp.mean(x * x, axis=-1, keepdims=True) + RMS_EPS)
    for g_ref, o_ref in zip(g_refs, o_refs):
        o_ref[...] = (y * g_ref[...]).astype(o_ref.dtype)


def rmsnorm(x, gains, dtypes):
    m, d = x.shape
    tm = min(NORM_TILE, m)
    n = len(gains)
    row = pl.BlockSpec((tm, d), lambda i: (i, 0))
    vec = pl.BlockSpec((1, d), lambda i: (0, 0))
    outs = pl.pallas_call(
        functools.partial(_rms_body, n),
        grid=(m // tm,),
        in_specs=[row] + [vec] * n,
        out_specs=[row] * n,
        out_shape=[jax.ShapeDtypeStruct((m, d), dt) for dt in dtypes],
        compiler_params=_params(("arbitrary",), tm * d * 4 * (2 * (1 + n) + 3)),
        name="rmsnorm",
    )(x, *[g.reshape(1, d).astype(F32) for g in gains])
    return outs


def _rms_final_body(x_ref, g_ref, op_ref, os_ref, *, n_prompt_blocks):
    x = x_ref[...]
    y = x * lax.rsqrt(jnp.mean(x * x, axis=-1, keepdims=True) + RMS_EPS) * g_ref[...]
    i = pl.program_id(0)

    @pl.when(i < n_prompt_blocks)
    def _():
        op_ref[...] = y

    @pl.when(i >= n_prompt_blocks)
    def _():
        os_ref[...] = y


def rmsnorm_final(x, gain, n_prompt):
    m, d = x.shape
    tm = math.gcd(NORM_TILE, math.gcd(n_prompt, m - n_prompt))
    npb = n_prompt // tm
    return pl.pallas_call(
        functools.partial(_rms_final_body, n_prompt_blocks=npb),
        grid=(m // tm,),
        in_specs=[pl.BlockSpec((tm, d), lambda i: (i, 0)), pl.BlockSpec((1, d), lambda i: (0, 0))],
        out_specs=[pl.BlockSpec((tm, d), lambda i: (jnp.minimum(i, npb - 1), 0)),
                   pl.BlockSpec((tm, d), lambda i: (jnp.maximum(i - npb, 0), 0))],
        out_shape=[jax.ShapeDtypeStruct((n_prompt, d), F32), jax.ShapeDtypeStruct((m - n_prompt, d), F32)],
        compiler_params=_params(("arbitrary",), 9 * tm * d * 4),
        name="rmsnorm_final",
    )(x, gain.reshape(1, d).astype(F32))


def _rms_router_body(x_ref, g_ref, wr_ref, br_ref, xn_ref, lg_ref):
    x = x_ref[...]
    y = x * lax.rsqrt(jnp.mean(x * x, axis=-1, keepdims=True) + RMS_EPS)
    xn = (y * g_ref[...]).astype(BF16)
    xn_ref[...] = xn
    lg_ref[...] = jnp.dot(xn, wr_ref[...], preferred_element_type=F32) + br_ref[...]


def rmsnorm_router(x, gain, w_router, b_router):
    m, d = x.shape
    tm = min(NORM_TILE, m)
    nr = w_router.shape[1]
    return pl.pallas_call(
        _rms_router_body,
        grid=(m // tm,),
        in_specs=[pl.BlockSpec((tm, d), lambda i: (i, 0)), pl.BlockSpec((1, d), lambda i: (0, 0)),
                  pl.BlockSpec((d, nr), lambda i: (0, 0)), pl.BlockSpec((1, nr), lambda i: (0, 0))],
        out_specs=[pl.BlockSpec((tm, d), lambda i: (i, 0)), pl.BlockSpec((tm, nr), lambda i: (i, 0))],
        out_shape=[jax.ShapeDtypeStruct((m, d), BF16), jax.ShapeDtypeStruct((m, nr), F32)],
        compiler_params=_params(("arbitrary",), tm * d * (2 * 6 + 3 * 4) + 4 * d * nr),
        name="rmsnorm_router",
    )(x, gain.reshape(1, d).astype(F32), w_router, b_router)


def _mm_body(glu, has_res, x_ref, *refs):
    nw = 2 if glu else 1
    w_refs = refs[:nw]
    res_ref = refs[nw] if has_res else None
    o_ref = refs[nw + (1 if has_res else 0)]
    wb_refs = refs[nw + (1 if has_res else 0) + 1:]

    @pl.when(pl.program_id(1) == 0)
    def _():
        for w_ref, wb_ref in zip(w_refs, wb_refs):
            wb_ref[...] = w_ref[...].astype(BF16)

    x = x_ref[...]
    acc = jnp.dot(x, wb_refs[0][...], preferred_element_type=F32)
    if glu:
        gate = jnp.dot(x, wb_refs[1][...], preferred_element_type=F32)
        acc = acc * jax.nn.sigmoid(gate)
    if has_res:
        acc = res_ref[...] + acc
    o_ref[...] = acc.astype(o_ref.dtype)


def matmul(x, w, prefix=(), n_out=None, glu=False, res=None, out_dtype=F32, name="matmul"):
    m, k = x.shape
    n = n_out if n_out is not None else w.shape[-1]
    tm = min(ROW_TILE, m)
    tn = min(COL_TILE // 2 if glu else COL_TILE, n)
    lead = (None,) * len(prefix)
    nblk = n // tn
    w_specs = [pl.BlockSpec(lead + (k, tn), lambda j, i: prefix + (0, j))]
    if glu:
        w_specs.append(pl.BlockSpec(lead + (k, tn), lambda j, i: prefix + (0, nblk + j)))
    nw = len(w_specs)
    in_specs = [pl.BlockSpec((tm, k), lambda j, i: (i, 0))] + w_specs
    args = [x] + [w] * nw
    if res is not None:
        in_specs.append(pl.BlockSpec((tm, tn), lambda j, i: (i, j)))
        args.append(res)
    osize = jnp.dtype(out_dtype).itemsize
    vmem = 2 * tm * k * 2 + nw * (2 * k * tn * 4 + k * tn * 2) + 2 * tm * tn * (osize + 4) + 4 * tm * tn * 4
    return pl.pallas_call(
        functools.partial(_mm_body, glu, res is not None),
        grid=(nblk, m // tm),
        in_specs=in_specs,
        out_specs=pl.BlockSpec((tm, tn), lambda j, i: (i, j)),
        out_shape=jax.ShapeDtypeStruct((m, n), out_dtype),
        scratch_shapes=[pltpu.VMEM((k, tn), BF16)] * nw,
        compiler_params=_params(("arbitrary", "arbitrary"), vmem),
        name=name,
    )(*args)


def _ssm_prep(a_re, a_im, log_dt, b_re, b_im, c_re, c_im):
    hi = lax.Precision.HIGHEST
    ls = SSM_SUB
    a_re = a_re.astype(F32)
    a_im = a_im.astype(F32)
    dt = jnp.exp(log_dt.astype(F32))[:, None]
    mag = jnp.exp(a_re * dt)
    ab_re = mag * jnp.cos(a_im * dt)
    ab_im = mag * jnp.sin(a_im * dt)
    den = a_re * a_re + a_im * a_im
    zr = ab_re - 1.0
    f_re = (zr * a_re + ab_im * a_im) / den
    f_im = (ab_im * a_re - zr * a_im) / den
    b_re = b_re.astype(F32)
    b_im = b_im.astype(F32)
    bb_re = f_re[..., None] * b_re - f_im[..., None] * b_im
    bb_im = f_re[..., None] * b_im + f_im[..., None] * b_re
    c_re = c_re.astype(F32)
    c_im = c_im.astype(F32)

    pw_re = [jnp.ones_like(ab_re)]
    pw_im = [jnp.zeros_like(ab_im)]
    for _ in range(ls):
        pr, pi = pw_re[-1], pw_im[-1]
        pw_re.append(pr * ab_re - pi * ab_im)
        pw_im.append(pr * ab_im + pi * ab_re)
    p_re = jnp.stack(pw_re, axis=1)
    p_im = jnp.stack(pw_im, axis=1)
    g = a_re.shape[0]

    cp_re = c_re[:, None] * p_re[:, :, None, :] - c_im[:, None] * p_im[:, :, None, :]
    cp_im = c_re[:, None] * p_im[:, :, None, :] + c_im[:, None] * p_re[:, :, None, :]
    kern = (jnp.einsum('gtpn,gnq->gtpq', cp_re[:, :ls], bb_re, precision=hi)
            - jnp.einsum('gtpn,gnq->gtpq', cp_im[:, :ls], bb_im, precision=hi))
    s_idx = jnp.arange(ls)[:, None]
    t_idx = jnp.arange(ls)[None, :]
    lag = t_idx - s_idx
    toep = kern[:, jnp.clip(lag, 0, ls - 1)]
    toep = jnp.where((lag >= 0)[None, :, :, None, None], toep, 0.0)
    toep = toep.transpose(0, 1, 4, 2, 3).reshape(g, ls * SSM_GROUP, ls * SSM_GROUP)

    pr_rev = p_re[:, ls - 1::-1][:, :, None, :] if ls > 0 else None
    pi_rev = p_im[:, ls - 1::-1][:, :, None, :]
    bbt_re = bb_re.transpose(0, 2, 1)[:, None]
    bbt_im = bb_im.transpose(0, 2, 1)[:, None]
    win_re = pr_rev * bbt_re - pi_rev * bbt_im
    win_im = pr_rev * bbt_im + pi_rev * bbt_re
    win = jnp.concatenate([win_re, win_im], axis=-1).reshape(g, ls * SSM_GROUP, 2 * SSM_STATE)

    wo_re = cp_re[:, 1:].transpose(0, 3, 1, 2).reshape(g, SSM_STATE, ls * SSM_GROUP)
    wo_im = -cp_im[:, 1:].transpose(0, 3, 1, 2).reshape(g, SSM_STATE, ls * SSM_GROUP)
    wout = jnp.concatenate([wo_re, wo_im], axis=1)

    al_re, al_im = p_re[:, ls], p_im[:, ls]
    a_rows = jnp.stack([jnp.concatenate([al_re, al_re], -1), jnp.concatenate([-al_im, al_im], -1)], axis=1)
    return toep.astype(BF16), win.astype(BF16), wout.astype(BF16), a_rows


def _ssm_body(rp, ns, nb, u_ref, t_ref, win_ref, wout_ref, a_ref, h0_ref, y_ref, hp_ref, hs_ref, v_s, hin_s):
    gb = u_ref.shape[0]
    for g in range(gb):
        v_s[g] = jnp.dot(u_ref[g], win_ref[g], preferred_element_type=F32)
    a1 = a_ref[:, 0:1, :]
    a2 = a_ref[:, 1:2, :]

    def step(h, v):
        return h * a1 + pltpu.roll(h, SSM_STATE, 2) * a2 + v

    def body(c, h):
        hin_s[:, pl.ds(c, 1), :] = h
        return step(h, v_s[:, pl.ds(c, 1), :])

    h = lax.fori_loop(0, rp, body, jnp.zeros((gb, 1, 2 * SSM_STATE), F32))
    hp_ref[...] = h
    h = h0_ref[...]
    for j in range(ns):
        r0 = rp + j * nb
        hin_s[:, r0:r0 + nb, :] = h
        h = step(h, v_s[:, r0:r0 + nb, :])
    hs_ref[...] = h
    for g in range(gb):
        y_ref[g] = (jnp.dot(u_ref[g], t_ref[g], preferred_element_type=F32)
                    + jnp.dot(hin_s[g].astype(BF16), wout_ref[g], preferred_element_type=F32))


def ssm_mix(xn, n_prompt, n_batch, h0_re, h0_im, prep):
    rows, d = xn.shape
    g = d // SSM_GROUP
    ls = SSM_SUB
    t_dec = (rows - n_prompt) // n_batch
    rp = n_prompt // ls
    ns = t_dec // ls
    r = rp + ns * n_batch
    lw = ls * SSM_GROUP
    toep, win, wout, a_rows = prep

    up = xn[:n_prompt].reshape(rp, ls, g, SSM_GROUP).transpose(2, 0, 1, 3).reshape(g, rp, lw)
    us = xn[n_prompt:].reshape(n_batch, ns, ls, g, SSM_GROUP).transpose(3, 1, 0, 2, 4).reshape(g, ns * n_batch, lw)
    u = jnp.concatenate([up, us], axis=1).astype(BF16)
    h0 = jnp.concatenate([h0_re.astype(F32), h0_im.astype(F32)], axis=-1).transpose(1, 0, 2)

    gb = min(SSM_GROUPS_PER_STEP, g)
    sn = 2 * SSM_STATE
    blk = lambda *s: pl.BlockSpec((gb,) + s, lambda i: (i, 0, 0))
    vmem = 2 * gb * (r * lw * 2 + lw * lw * 2 + 2 * lw * sn * 2 + r * lw * 4) + 2 * gb * r * sn * 4
    y, hp, hs = pl.pallas_call(
        functools.partial(_ssm_body, rp, ns, n_batch),
        grid=(g // gb,),
        in_specs=[blk(r, lw), blk(lw, lw), blk(lw, sn), blk(sn, lw), blk(2, sn), blk(n_batch, sn)],
        out_specs=[blk(r, lw), blk(1, sn), blk(n_batch, sn)],
        out_shape=[jax.ShapeDtypeStruct((g, r, lw), F32), jax.ShapeDtypeStruct((g, 1, sn), F32),
                   jax.ShapeDtypeStruct((g, n_batch, sn), F32)],
        scratch_shapes=[pltpu.VMEM((gb, r, sn), F32), pltpu.VMEM((gb, r, sn), F32)],
        compiler_params=_params(("arbitrary",), vmem),
        name="ssm",
    )(u, toep, win, wout, a_rows, h0)

    yp = y[:, :rp].reshape(g, rp, ls, SSM_GROUP).transpose(1, 2, 0, 3).reshape(n_prompt, d)
    ys = y[:, rp:].reshape(g, ns, n_batch, ls, SSM_GROUP).transpose(2, 1, 3, 0, 4).reshape(rows - n_prompt, d)
    y_rows = jnp.concatenate([yp, ys], axis=0)
    hp = hp[:, 0, :]
    re_p = hp[:, :SSM_STATE][None]
    im_p = hp[:, SSM_STATE:][None]
    hs = hs.transpose(1, 0, 2)
    return y_rows, re_p, im_p, hs[..., :SSM_STATE], hs[..., SSM_STATE:]


def _gelu_body(y_ref, xn_ref, d_ref, o_ref):
    y = y_ref[...] + d_ref[...] * xn_ref[...]
    c = math.sqrt(2.0 / math.pi)
    cdf = 0.5 * (1.0 + jnp.tanh(c * (y + 0.044715 * (y * y * y))))
    o_ref[...] = (y * cdf).astype(o_ref.dtype)


def skip_gelu(y, xn, d_skip):
    m, d = y.shape
    tm = min(NORM_TILE, m)
    row = pl.BlockSpec((tm, d), lambda i: (i, 0))
    return pl.pallas_call(
        _gelu_body,
        grid=(m // tm,),
        in_specs=[row, row, pl.BlockSpec((1, d), lambda i: (0, 0))],
        out_specs=row,
        out_shape=jax.ShapeDtypeStruct((m, d), BF16),
        compiler_params=_params(("arbitrary",), tm * d * (2 * 10 + 4 * 4)),
        name="skip_gelu",
    )(y, xn, d_skip.reshape(1, d).astype(F32))


_NT = (((1,), (1,)), ((), ()))


def _softmax_rows(s):
    m = jnp.max(s, axis=-1, keepdims=True)
    e = jnp.exp(s - m)
    return e / jnp.sum(e, axis=-1, keepdims=True)


def _memattn_body(kv_per_step, q_ref, k_ref, v_ref, *refs):
    o_ref, kb, vb = refs[-3:]

    def load_kv():
        kb[...] = k_ref[...].astype(BF16)
        vb[...] = v_ref[...].astype(BF16)

    if kv_per_step:
        load_kv()
    else:
        pl.when(pl.program_id(0) == 0)(load_kv)
    dh = q_ref.shape[1] // MEM_HEADS
    scale = dh ** -0.5
    for h in range(MEM_HEADS):
        sl = slice(h * dh, (h + 1) * dh)
        s = lax.dot_general(q_ref[:, sl], kb[:, sl], _NT, preferred_element_type=F32) * scale
        p = _softmax_rows(s).astype(BF16)
        o_ref[:, sl] = jnp.dot(p, vb[:, sl], preferred_element_type=F32).astype(o_ref.dtype)


def mem_attention(q, kv_prompt, cache_k, cache_v, layer, n_prompt, n_batch):
    rows, d = q.shape
    mt = kv_prompt.shape[0]
    tm = min(ROW_TILE, n_prompt)
    t_dec = (rows - n_prompt) // n_batch
    scratch = [pltpu.VMEM((mt, d), BF16), pltpu.VMEM((mt, d), BF16)]
    vmem_kv = 2 * 2 * mt * d * 4 + 2 * mt * d * 2
    o = pl.pallas_call(
        functools.partial(_memattn_body, False),
        grid=(n_prompt // tm,),
        in_specs=[pl.BlockSpec((tm, d), lambda i: (i, 0)),
                  pl.BlockSpec((mt, d), lambda i: (0, 0)),
                  pl.BlockSpec((mt, d), lambda i: (0, 1))],
        out_specs=pl.BlockSpec((tm, d), lambda i: (i, 0)),
        out_shape=jax.ShapeDtypeStruct((rows, d), BF16),
        scratch_shapes=scratch,
        compiler_params=_params(("arbitrary",), vmem_kv + 4 * tm * d * 2 + 2 * tm * d * 4),
        name="mem_attn_prompt",
    )(q, kv_prompt, kv_prompt)
    pb = n_prompt // t_dec
    ck = cache_k.reshape(cache_k.shape[0], n_batch, mt, d)
    cv = cache_v.reshape(cache_v.shape[0], n_batch, mt, d)
    return pl.pallas_call(
        functools.partial(_memattn_body, True),
        grid=(n_batch,),
        in_specs=[pl.BlockSpec((t_dec, d), lambda b: (pb + b, 0)),
                  pl.BlockSpec((None, None, mt, d), lambda b: (layer, b, 0, 0)),
                  pl.BlockSpec((None, None, mt, d), lambda b: (layer, b, 0, 0)),
                  pl.BlockSpec(memory_space=pl.ANY)],
        out_specs=pl.BlockSpec((t_dec, d), lambda b: (pb + b, 0)),
        out_shape=jax.ShapeDtypeStruct((rows, d), BF16),
        scratch_shapes=scratch,
        input_output_aliases={3: 0},
        compiler_params=_params(("arbitrary",), vmem_kv + 4 * t_dec * d * 6),
        name="mem_attn_sample",
    )(q, ck, cv, o)


def _band_prompt_body(ncb, q_ref, k_ref, v_ref, b_ref, o_ref, kb, vb):
    qb = pl.program_id(1)
    s_len, dh = k_ref.shape
    past = PAST_CHUNKS * CHUNK

    @pl.when(qb == 0)
    def _():
        for ref, src in ((kb, k_ref), (vb, v_ref)):
            ref[0:past, :] = jnp.zeros((past, dh), BF16)
            ref[past + s_len:past + s_len + CHUNK, :] = jnp.zeros((CHUNK, dh), BF16)
            ref[past:past + s_len, :] = src[...].astype(BF16)

    bias = b_ref[...]
    col = lax.broadcasted_iota(jnp.int32, (CHUNK, BAND_WIN), 1)
    scale = dh ** -0.5

    def chunk(ci, carry):
        c = qb * ncb + ci
        r0 = pl.multiple_of(ci * CHUNK, CHUNK)
        w0 = pl.multiple_of(c * CHUNK, CHUNK)
        q = q_ref[pl.ds(r0, CHUNK), :]
        s = lax.dot_general(q, kb[pl.ds(w0, BAND_WIN), :], _NT, preferred_element_type=F32) * scale + bias
        s = jnp.where(col >= (PAST_CHUNKS - c) * CHUNK, s, NEG)
        p = _softmax_rows(s).astype(BF16)
        o_ref[pl.ds(r0, CHUNK), :] = jnp.dot(p, vb[pl.ds(w0, BAND_WIN), :],
                                             preferred_element_type=F32).astype(o_ref.dtype)
        return carry

    lax.fori_loop(0, ncb, chunk, 0)


def _band_sample_body(hb, dh, q_ref, kc_ref, vc_ref, kn_ref, vn_ref, bc_ref, bn_ref, prev_ref, o_ref):
    del prev_ref
    scale = dh ** -0.5
    for h in range(hb):
        sl = slice(h * dh, (h + 1) * dh)
        q = q_ref[:, sl]
        s1 = lax.dot_general(q, kc_ref[:, sl].astype(BF16), _NT, preferred_element_type=F32) * scale + bc_ref[h]
        s2 = lax.dot_general(q, kn_ref[:, sl].astype(BF16), _NT, preferred_element_type=F32) * scale + bn_ref[h]
        m = jnp.maximum(jnp.max(s1, axis=-1, keepdims=True), jnp.max(s2, axis=-1, keepdims=True))
        e1 = jnp.exp(s1 - m)
        e2 = jnp.exp(s2 - m)
        den = jnp.sum(e1, axis=-1, keepdims=True) + jnp.sum(e2, axis=-1, keepdims=True)
        o = (jnp.dot((e1 / den).astype(BF16), vc_ref[:, sl].astype(BF16), preferred_element_type=F32)
             + jnp.dot((e2 / den).astype(BF16), vn_ref[:, sl].astype(BF16), preferred_element_type=F32))
        o_ref[:, sl] = o.astype(o_ref.dtype)


def band_attention(q, kv, cache_k, cache_v, table, n_prompt, n_batch):
    rows, d = q.shape
    nh = N_HEADS
    dh = d // nh
    past = PAST_CHUNKS * CHUNK
    band = past + CHUNK
    t_dec = (rows - n_prompt) // n_batch
    cb = cache_k.shape[1]

    rel = past + jnp.arange(CHUNK)[:, None] - jnp.arange(band)[None, :]
    bias = table[:, jnp.clip(rel, -MAX_REL, MAX_REL) + MAX_REL].astype(F32)
    bias_p = jnp.pad(bias, ((0, 0), (0, 0), (0, BAND_WIN - band)), constant_values=NEG)
    rel_s = cb + jnp.arange(t_dec)[:, None] - jnp.arange(cb + t_dec)[None, :]
    bias_s = table[:, jnp.clip(rel_s, -MAX_REL, MAX_REL) + MAX_REL].astype(F32)
    bias_c, bias_n = bias_s[:, :, :cb], bias_s[:, :, cb:]

    bq = min(BAND_Q_ROWS, n_prompt)
    ncb = bq // CHUNK
    pad_rows = past + n_prompt + CHUNK
    o = pl.pallas_call(
        functools.partial(_band_prompt_body, ncb),
        grid=(nh, n_prompt // bq),
        in_specs=[pl.BlockSpec((bq, dh), lambda h, i: (i, h)),
                  pl.BlockSpec((n_prompt, dh), lambda h, i: (0, h)),
                  pl.BlockSpec((n_prompt, dh), lambda h, i: (0, nh + h)),
                  pl.BlockSpec((None, CHUNK, BAND_WIN), lambda h, i: (h, 0, 0))],
        out_specs=pl.BlockSpec((bq, dh), lambda h, i: (i, h)),
        out_shape=jax.ShapeDtypeStruct((rows, d), BF16),
        scratch_shapes=[pltpu.VMEM((pad_rows, dh), BF16), pltpu.VMEM((pad_rows, dh), BF16)],
        compiler_params=_params(("arbitrary", "arbitrary"),
                                4 * n_prompt * dh * 4 + 2 * pad_rows * dh * 2 + 8 * bq * dh * 2),
        name="band_attn_prompt",
    )(q, kv, kv, bias_p)

    hb = min(SAMPLE_HEADS_PER_STEP, nh)
    ngrp = nh // hb
    pb = n_prompt // t_dec
    ck = cache_k.reshape(n_batch, cb, d)
    cv = cache_v.reshape(n_batch, cb, d)
    w = hb * dh
    return pl.pallas_call(
        functools.partial(_band_sample_body, hb, dh),
        grid=(n_batch, ngrp),
        in_specs=[pl.BlockSpec((t_dec, w), lambda b, g: (pb + b, g)),
                  pl.BlockSpec((None, cb, w), lambda b, g: (b, 0, g)),
                  pl.BlockSpec((None, cb, w), lambda b, g: (b, 0, g)),
                  pl.BlockSpec((t_dec, w), lambda b, g: (pb + b, g)),
                  pl.BlockSpec((t_dec, w), lambda b, g: (pb + b, ngrp + g)),
                  pl.BlockSpec((hb, t_dec, cb), lambda b, g: (g, 0, 0)),
                  pl.BlockSpec((hb, t_dec, t_dec), lambda b, g: (g, 0, 0)),
                  pl.BlockSpec(memory_space=pl.ANY)],
        out_specs=pl.BlockSpec((t_dec, w), lambda b, g: (pb + b, g)),
        out_shape=jax.ShapeDtypeStruct((rows, d), BF16),
        input_output_aliases={7: 0},
        compiler_params=_params(("arbitrary", "arbitrary"), 4 * cb * w * 4 + 16 * t_dec * w * 4 + 4 * hb * t_dec * cb * 4),
        name="band_attn_sample",
    )(q, ck, cv, kv, kv, bias_c, bias_n, o)


def _moe_body(item_e, item_blk, item_nsub, x_ref, wg_ref, wu_ref, wo_ref, o_ref, wgb, wub, wob):
    del item_e, item_blk
    i = pl.program_id(0)
    j = pl.program_id(1)
    nsub = item_nsub[i]

    @pl.when(nsub > 0)
    def _():
        wgb[...] = wg_ref[...].astype(BF16)
        wub[...] = wu_ref[...].astype(BF16)
        wob[...] = wo_ref[...].astype(BF16)

        @pl.when(j == 0)
        def _():
            o_ref[...] = jnp.zeros_like(o_ref)

        def sub(s, carry):
            r0 = pl.multiple_of(s * MOE_SUB, MOE_SUB)
            xb = x_ref[pl.ds(r0, MOE_SUB), :]
            g = jnp.dot(xb, wgb[...], preferred_element_type=F32)
            u = jnp.dot(xb, wub[...], preferred_element_type=F32)
            h = (g * jax.nn.sigmoid(g) * u).astype(BF16)
            o_ref[pl.ds(r0, MOE_SUB), :] += jnp.dot(h, wob[...], preferred_element_type=F32)
            return carry

        lax.fori_loop(0, nsub, sub, 0)


def moe_experts(xs, w_in, w_out, layer, item_e, item_blk, item_nsub):
    n_slots, d = xs.shape
    de = w_out.shape[2]
    tj = min(MOE_COL, de)
    nj = de // tj
    n_items = n_slots // MOE_ROWS

    def jj(i, j, ie, ib, ns):
        return jnp.where(ns[i] > 0, j, nj - 1)

    grid_spec = pltpu.PrefetchScalarGridSpec(
        num_scalar_prefetch=3,
        grid=(n_items, nj),
        in_specs=[pl.BlockSpec((MOE_ROWS, d), lambda i, j, ie, ib, ns: (ib[i], 0)),
                  pl.BlockSpec((None, None, d, tj), lambda i, j, ie, ib, ns: (layer, ie[i], 0, jj(i, j, ie, ib, ns))),
                  pl.BlockSpec((None, None, d, tj),
                               lambda i, j, ie, ib, ns: (layer, ie[i], 0, nj + jj(i, j, ie, ib, ns))),
                  pl.BlockSpec((None, None, tj, d), lambda i, j, ie, ib, ns: (layer, ie[i], jj(i, j, ie, ib, ns), 0))],
        out_specs=pl.BlockSpec((MOE_ROWS, d), lambda i, j, ie, ib, ns: (ib[i], 0)),
        scratch_shapes=[pltpu.VMEM((d, tj), BF16), pltpu.VMEM((d, tj), BF16), pltpu.VMEM((tj, d), BF16)],
    )
    vmem = 2 * MOE_ROWS * d * (2 + 4) + 3 * d * tj * (2 * 4 + 2) + 4 * MOE_SUB * d * 4
    return pl.pallas_call(
        _moe_body,
        grid_spec=grid_spec,
        out_shape=jax.ShapeDtypeStruct((n_slots, d), F32),
        compiler_params=_params(("arbitrary", "arbitrary"), vmem),
        name="moe_experts",
    )(item_e, item_blk, item_nsub, xs, w_in, w_in, w_out)


def _route(logits):
    t = logits.shape[0]
    lg = logits[:, :N_GROUPS]
    _, g_idx = lax.top_k(lg, 1)
    p_group = jnp.take_along_axis(jax.nn.softmax(lg, axis=-1), g_idx, axis=-1)
    le = logits[:, N_GROUPS:N_GROUPS + N_EXPERTS].reshape(t, N_GROUPS, EXPERTS_PER_GROUP)
    le = jnp.take_along_axis(le, g_idx[:, :, None], axis=1)[:, 0]
    top_v, top_j = lax.top_k(le, TOP_K)
    gate = p_group * jax.nn.softmax(top_v, axis=-1)
    eidx = g_idx * EXPERTS_PER_GROUP + top_j
    return eidx.astype(jnp.int32), gate


def _dispatch(eidx):
    t = eidx.shape[0]
    a = t * TOP_K
    n_items = a // MOE_ROWS + N_EXPERTS
    flat_e = eidx.reshape(a)
    flat_tok = jnp.repeat(jnp.arange(t, dtype=jnp.int32), TOP_K)
    order = jnp.argsort(flat_e)
    se = flat_e[order]
    counts = jnp.zeros((N_EXPERTS,), jnp.int32).at[flat_e].add(1)
    nblk = (counts + MOE_ROWS - 1) // MOE_ROWS
    blk_end = jnp.cumsum(nblk)
    blk_start = blk_end - nblk
    start = jnp.cumsum(counts) - counts
    rank = jnp.arange(a, dtype=jnp.int32) - start[se]
    dest_sorted = blk_start[se] * MOE_ROWS + rank
    slot_tok = jnp.zeros((n_items * MOE_ROWS,), jnp.int32).at[dest_sorted].set(flat_tok[order])
    dest = jnp.zeros((a,), jnp.int32).at[order].set(dest_sorted).reshape(t, TOP_K)
    n_used = blk_end[-1]
    item = jnp.arange(n_items, dtype=jnp.int32)
    item_c = jnp.minimum(item, n_used - 1)
    item_e = jnp.minimum(jnp.searchsorted(blk_end, item_c, side='right'), N_EXPERTS - 1).astype(jnp.int32)
    rows_left = counts[item_e] - (item_c - blk_start[item_e]) * MOE_ROWS
    nsub = (jnp.clip(rows_left, 0, MOE_ROWS) + MOE_SUB - 1) // MOE_SUB
    item_nsub = jnp.where(item < n_used, nsub, 0).astype(jnp.int32)
    return slot_tok, dest, item_e, item_c.astype(jnp.int32), item_nsub


def moe_ffn(x, xn, logits, w_in, w_out, layer):
    eidx, gate = _route(logits)
    slot_tok, dest, item_e, item_blk, item_nsub = _dispatch(eidx)
    xs = xn[slot_tok]
    ys = moe_experts(xs, w_in, w_out, layer, item_e, item_blk, item_nsub)
    c0 = ys[dest[:, 0]] * gate[:, 0:1]
    c1 = ys[dest[:, 1]] * gate[:, 1:2]
    return x + (c0 + c1)


def _router_weights(w_group, b_group, w_expert, b_expert):
    d = w_group.shape[0]
    pad = LANES - N_GROUPS - N_EXPERTS
    w = jnp.concatenate([w_group, w_expert, jnp.zeros((d, pad), w_group.dtype)], axis=1).astype(BF16)
    b = jnp.concatenate([b_group.astype(F32), b_expert.astype(F32), jnp.zeros((pad,), F32)]).reshape(1, LANES)
    return w, b


def kernel(x_prompt, x_sample, state_ssm_re, state_ssm_im, cache_band_k, cache_band_v, cache_mem_k, cache_mem_v, mem_prompt, norm_mix, norm_mem, norm_memin, norm_ffn, norm_kv, norm_final, ssm_a_re, ssm_a_im, ssm_log_dt, ssm_b_re, ssm_b_im, ssm_c_re, ssm_c_im, ssm_d, ssm_w_glu, w_kv_shared, attn_w_q, attn_rel_bias, attn_w_o, mem_w_q, mem_w_kv, mem_w_o, moe_w_group, moe_b_group, moe_w_expert, moe_b_expert, moe_w_in, moe_w_out):
    bp, seq, d = x_prompt.shape
    nb, t_dec, _ = x_sample.shape
    assert bp == 1
    n_prompt = bp * seq
    depth = norm_mix.shape[0]
    n_a = ssm_a_re.shape[0]
    dh = d // N_HEADS
    mt = mem_prompt.shape[1]
    x = jnp.concatenate([x_prompt.reshape(n_prompt, d), x_sample.reshape(nb * t_dec, d)], axis=0)
    mem = mem_prompt.reshape(mt, d)

    out_re_p, out_im_p, out_re_s, out_im_s, out_mk, out_mv = [], [], [], [], [], []
    kv = None
    for layer in range(depth):
        if layer < n_a:
            a = layer
            (xn,) = rmsnorm(x, [norm_mix[layer]], [F32])
            prep = _ssm_prep(ssm_a_re[a], ssm_a_im[a], ssm_log_dt[a], ssm_b_re[a], ssm_b_im[a],
                             ssm_c_re[a], ssm_c_im[a])
            y, re_p, im_p, re_s, im_s = ssm_mix(xn, n_prompt, nb, state_ssm_re[a], state_ssm_im[a], prep)
            out_re_p.append(re_p)
            out_im_p.append(im_p)
            out_re_s.append(re_s)
            out_im_s.append(im_s)
            g = skip_gelu(y, xn, ssm_d[a])
            x = matmul(g, ssm_w_glu, prefix=(a,), n_out=d, glu=True, res=x, name="glu_proj")
        else:
            b = layer - n_a
            if layer == n_a:
                kvn, xn = rmsnorm(x, [norm_kv, norm_mix[layer]], [BF16, BF16])
                kv = matmul(kvn, w_kv_shared, name="kv_proj")
            else:
                (xn,) = rmsnorm(x, [norm_mix[layer]], [BF16])
            q = matmul(xn, attn_w_q, prefix=(b,), out_dtype=BF16, name="attn_q_proj")
            o = band_attention(q, kv, cache_band_k, cache_band_v, attn_rel_bias[b], n_prompt, nb)
            x = matmul(o, attn_w_o, prefix=(b,), res=x, name="attn_o_proj")

        (memn,) = rmsnorm(mem, [norm_memin[layer]], [BF16])
        kv_mem = matmul(memn, mem_w_kv, prefix=(layer,), name="mem_kv_proj")
        out_mk.append(kv_mem[:, :d].reshape(bp, mt, MEM_HEADS, d // MEM_HEADS))
        out_mv.append(kv_mem[:, d:].reshape(bp, mt, MEM_HEADS, d // MEM_HEADS))
        (xn,) = rmsnorm(x, [norm_mem[layer]], [BF16])
        q = matmul(xn, mem_w_q, prefix=(layer,), out_dtype=BF16, name="mem_q_proj")
        o = mem_attention(q, kv_mem, cache_mem_k, cache_mem_v, layer, n_prompt, nb)
        x = matmul(o, mem_w_o, prefix=(layer,), res=x, name="mem_o_proj")

        w_r, b_r = _router_weights(moe_w_group[layer], moe_b_group[layer], moe_w_expert[layer], moe_b_expert[layer])
        xn, logits = rmsnorm_router(x, norm_ffn[layer], w_r, b_r)
        x = moe_ffn(x, xn, logits, moe_w_in, moe_w_out, layer)

    y_p, y_s = rmsnorm_final(x, norm_final, n_prompt)
    keep = min(PAST_CHUNKS * CHUNK, seq)
    k_all, v_all = kv[:, :d], kv[:, d:]
    bk_p = k_all[n_prompt - keep:n_prompt].reshape(bp, keep, N_HEADS, dh)
    bv_p = v_all[n_prompt - keep:n_prompt].reshape(bp, keep, N_HEADS, dh)
    keep_s = min(PAST_CHUNKS * CHUNK, t_dec)
    bk_s = k_all[n_prompt:].reshape(nb, t_dec, N_HEADS, dh)[:, t_dec - keep_s:]
    bv_s = v_all[n_prompt:].reshape(nb, t_dec, N_HEADS, dh)[:, t_dec - keep_s:]
    return (y_p.reshape(bp, seq, d), y_s.reshape(nb, t_dec, d),
            jnp.stack(out_re_p), jnp.stack(out_im_p), jnp.stack(out_re_s), jnp.stack(out_im_s),
            bk_p, bv_p, bk_s, bv_s, jnp.stack(out_mk), jnp.stack(out_mv))
```

```python
import functools
import math

import jax
import jax.numpy as jnp
from jax import lax
from jax.experimental import pallas as pl
from jax.experimental.pallas import tpu as pltpu

F32 = jnp.float32
BF16 = jnp.bfloat16

CHUNK = 64
RMS_EPS = 1e-6
SSM_GROUP = 16
SSM_STATE = 64
N_HEADS = 32
PAST_CHUNKS = 8
MAX_REL = 256
MEM_HEADS = 4
N_GROUPS = 8
EXPERTS_PER_GROUP = 8
N_EXPERTS = N_GROUPS * EXPERTS_PER_GROUP
TOP_K = 2

V7X_VMEM_BYTES = 64 * 1024 * 1024
LANES = 128
MXU_DIM = 256

ROW_TILE = 512
NORM_TILE = 256
COL_TILE = 512
SSM_SUB = 8
SSM_TILE_GROUPS = LANES // SSM_GROUP
BAND_Q_ROWS = 1024
BAND_WIN = (PAST_CHUNKS + 2) * CHUNK
BAND_UNROLL = 4
SAMPLE_HEADS_PER_STEP = 8
MOE_ROWS = 384
MOE_SUB = 128
MOE_COL = 256
COMBINE_ROWS = 128
NEG = -1e30


def _params(sem, vmem_bytes):
    limit = min(int(vmem_bytes) + (4 << 20), V7X_VMEM_BYTES - (8 << 20))
    return pltpu.CompilerParams(dimension_semantics=sem, vmem_limit_bytes=limit)


def _rms_body(n_out, x_ref, *refs):
    g_refs, o_refs = refs[:n_out], refs[n_out:]
    x = x_ref[...]
    y = x * lax.rsqrt(jnp.mean(x * x, axis=-1, keepdims=True) + RMS_EPS)
    for g_ref, o_ref in zip(g_refs, o_refs):
        o_ref[...] = (y * g_ref[...]).astype(o_ref.dtype)


def rmsnorm(x, gains, dtypes):
    m, d = x.shape
    tm = min(NORM_TILE, m)
    n = len(gains)
    row = pl.BlockSpec((tm, d), lambda i: (i, 0))
    vec = pl.BlockSpec((1, d), lambda i: (0, 0))
    outs = pl.pallas_call(
        functools.partial(_rms_body, n),
        grid=(m // tm,),
        in_specs=[row] + [vec] * n,
        out_specs=[row] * n,
        out_shape=[jax.ShapeDtypeStruct((m, d), dt) for dt in dtypes],
        compiler_params=_params(("arbitrary",), tm * d * 4 * (2 * (1 + n) + 3)),
        name="rmsnorm",
    )(x, *[g.reshape(1, d).astype(F32) for g in gains])
    return outs


def _rms_final_body(x_ref, g_ref, op_ref, os_ref, *, n_prompt_blocks):
    x = x_ref[...]
    y = x * lax.rsqrt(jnp.mean(x * x, axis=-1, keepdims=True) + RMS_EPS) * g_ref[...]
    i = pl.program_id(0)

    @pl.when(i < n_prompt_blocks)
    def _():
        op_ref[...] = y

    @pl.when(i >= n_prompt_blocks)
    def _():
        os_ref[...] = y


def rmsnorm_final(x, gain, n_prompt):
    m, d = x.shape
    tm = math.gcd(NORM_TILE, math.gcd(n_prompt, m - n_prompt))
    npb = n_prompt // tm
    return pl.pallas_call(
        functools.partial(_rms_final_body, n_prompt_blocks=npb),
        grid=(m // tm,),
        in_specs=[pl.BlockSpec((tm, d), lambda i: (i, 0)), pl.BlockSpec((1, d), lambda i: (0, 0))],
        out_specs=[pl.BlockSpec((tm, d), lambda i: (jnp.minimum(i, npb - 1), 0)),
                   pl.BlockSpec((tm, d), lambda i: (jnp.maximum(i - npb, 0), 0))],
        out_shape=[jax.ShapeDtypeStruct((n_prompt, d), F32), jax.ShapeDtypeStruct((m - n_prompt, d), F32)],
        compiler_params=_params(("arbitrary",), 9 * tm * d * 4),
        name="rmsnorm_final",
    )(x, gain.reshape(1, d).astype(F32))


def _rms_router_body(x_ref, g_ref, wr_ref, br_ref, xn_ref, lg_ref):
    x = x_ref[...]
    y = x * lax.rsqrt(jnp.mean(x * x, axis=-1, keepdims=True) + RMS_EPS)
    xn = y * g_ref[...]
    xn_ref[...] = xn
    lg_ref[...] = jnp.dot(xn.astype(BF16), wr_ref[...], preferred_element_type=F32) + br_ref[...]


def rmsnorm_router(x, gain, w_router, b_router):
    m, d = x.shape
    tm = min(NORM_TILE, m)
    nr = w_router.shape[1]
    return pl.pallas_call(
        _rms_router_body,
        grid=(m // tm,),
        in_specs=[pl.BlockSpec((tm, d), lambda i: (i, 0)), pl.BlockSpec((1, d), lambda i: (0, 0)),
                  pl.BlockSpec((d, nr), lambda i: (0, 0)), pl.BlockSpec((1, nr), lambda i: (0, 0))],
        out_specs=[pl.BlockSpec((tm, d), lambda i: (i, 0)), pl.BlockSpec((tm, nr), lambda i: (i, 0))],
        out_shape=[jax.ShapeDtypeStruct((m, d), F32), jax.ShapeDtypeStruct((m, nr), F32)],
        compiler_params=_params(("arbitrary",), tm * d * (2 * 8 + 3 * 4) + 4 * d * nr),
        name="rmsnorm_router",
    )(x, gain.reshape(1, d).astype(F32), w_router, b_router)


def _mm_body(glu, has_res, x_ref, *refs):
    nw = 2 if glu else 1
    w_refs = refs[:nw]
    res_ref = refs[nw] if has_res else None
    o_ref = refs[nw + (1 if has_res else 0)]
    wb_refs = refs[nw + (1 if has_res else 0) + 1:]

    @pl.when(pl.program_id(1) == 0)
    def _():
        for w_ref, wb_ref in zip(w_refs, wb_refs):
            wb_ref[...] = w_ref[...].astype(BF16)

    x = x_ref[...]
    acc = jnp.dot(x, wb_refs[0][...], preferred_element_type=F32)
    if glu:
        gate = jnp.dot(x, wb_refs[1][...], preferred_element_type=F32)
        acc = acc * jax.nn.sigmoid(gate)
    if has_res:
        acc = res_ref[...] + acc
    o_ref[...] = acc.astype(o_ref.dtype)


def matmul(x, w, prefix=(), n_out=None, glu=False, res=None, out_dtype=F32, name="matmul"):
    m, k = x.shape
    n = n_out if n_out is not None else w.shape[-1]
    tm = min(ROW_TILE, m)
    tn = min(COL_TILE // 2 if glu else COL_TILE, n)
    lead = (None,) * len(prefix)
    nblk = n // tn
    w_specs = [pl.BlockSpec(lead + (k, tn), lambda j, i: prefix + (0, j))]
    if glu:
        w_specs.append(pl.BlockSpec(lead + (k, tn), lambda j, i: prefix + (0, nblk + j)))
    nw = len(w_specs)
    in_specs = [pl.BlockSpec((tm, k), lambda j, i: (i, 0))] + w_specs
    args = [x] + [w] * nw
    if res is not None:
        in_specs.append(pl.BlockSpec((tm, tn), lambda j, i: (i, j)))
        args.append(res)
    osize = jnp.dtype(out_dtype).itemsize
    vmem = 2 * tm * k * 2 + nw * (2 * k * tn * 4 + k * tn * 2) + 2 * tm * tn * (osize + 4) + 4 * tm * tn * 4
    return pl.pallas_call(
        functools.partial(_mm_body, glu, res is not None),
        grid=(nblk, m // tm),
        in_specs=in_specs,
        out_specs=pl.BlockSpec((tm, tn), lambda j, i: (i, j)),
        out_shape=jax.ShapeDtypeStruct((m, n), out_dtype),
        scratch_shapes=[pltpu.VMEM((k, tn), BF16)] * nw,
        compiler_params=_params(("arbitrary", "arbitrary"), vmem),
        name=name,
    )(*args)


def _ssm_prep(a_re, a_im, log_dt, b_re, b_im, c_re, c_im):
    hi = lax.Precision.HIGHEST
    ls = SSM_SUB
    a_re = a_re.astype(F32)
    a_im = a_im.astype(F32)
    dt = jnp.exp(log_dt.astype(F32))[:, None]
    mag = jnp.exp(a_re * dt)
    ab_re = mag * jnp.cos(a_im * dt)
    ab_im = mag * jnp.sin(a_im * dt)
    den = a_re * a_re + a_im * a_im
    zr = ab_re - 1.0
    f_re = (zr * a_re + ab_im * a_im) / den
    f_im = (ab_im * a_re - zr * a_im) / den
    b_re = b_re.astype(F32)
    b_im = b_im.astype(F32)
    bb_re = f_re[..., None] * b_re - f_im[..., None] * b_im
    bb_im = f_re[..., None] * b_im + f_im[..., None] * b_re
    c_re = c_re.astype(F32)
    c_im = c_im.astype(F32)

    pw_re = [jnp.ones_like(ab_re)]
    pw_im = [jnp.zeros_like(ab_im)]
    for _ in range(ls):
        pr, pi = pw_re[-1], pw_im[-1]
        pw_re.append(pr * ab_re - pi * ab_im)
        pw_im.append(pr * ab_im + pi * ab_re)
    p_re = jnp.stack(pw_re, axis=1)
    p_im = jnp.stack(pw_im, axis=1)
    g = a_re.shape[0]

    cp_re = c_re[:, None] * p_re[:, :, None, :] - c_im[:, None] * p_im[:, :, None, :]
    cp_im = c_re[:, None] * p_im[:, :, None, :] + c_im[:, None] * p_re[:, :, None, :]
    kern = (jnp.einsum('gtpn,gnq->gtpq', cp_re[:, :ls], bb_re, precision=hi)
            - jnp.einsum('gtpn,gnq->gtpq', cp_im[:, :ls], bb_im, precision=hi))
    s_idx = jnp.arange(ls)[:, None]
    t_idx = jnp.arange(ls)[None, :]
    lag = t_idx - s_idx
    toep = kern[:, jnp.clip(lag, 0, ls - 1)]
    toep = jnp.where((lag >= 0)[None, :, :, None, None], toep, 0.0)
    toep = toep.transpose(0, 1, 4, 2, 3).reshape(g, ls * SSM_GROUP, ls * SSM_GROUP)

    pr_rev = p_re[:, ls - 1::-1][:, :, None, :]
    pi_rev = p_im[:, ls - 1::-1][:, :, None, :]
    bbt_re = bb_re.transpose(0, 2, 1)[:, None]
    bbt_im = bb_im.transpose(0, 2, 1)[:, None]
    win_re = pr_rev * bbt_re - pi_rev * bbt_im
    win_im = pr_rev * bbt_im + pi_rev * bbt_re
    win = jnp.concatenate([win_re, win_im], axis=-1).reshape(g, ls * SSM_GROUP, 2 * SSM_STATE)

    wo_re = cp_re[:, 1:].transpose(0, 3, 1, 2).reshape(g, SSM_STATE, ls * SSM_GROUP)
    wo_im = -cp_im[:, 1:].transpose(0, 3, 1, 2).reshape(g, SSM_STATE, ls * SSM_GROUP)
    wout = jnp.concatenate([wo_re, wo_im], axis=1)

    tg = SSM_TILE_GROUPS
    nt = g // tg
    eye = jnp.eye(tg, dtype=BF16)
    t6 = toep.astype(BF16).reshape(nt, tg, ls, SSM_GROUP, ls, SSM_GROUP).transpose(0, 2, 1, 3, 4, 5)
    t_tile = (t6[:, :, :, :, :, None, :] * eye[None, None, :, None, None, :, None]
              ).reshape(nt, ls * LANES, ls * LANES)
    w6 = win.astype(BF16).reshape(nt, tg, ls, SSM_GROUP, 2, SSM_STATE).transpose(0, 2, 1, 3, 4, 5)
    win_tile = (w6[:, :, :, :, :, None, :] * eye[None, None, :, None, None, :, None]
                ).reshape(nt, ls * LANES, 2 * tg * SSM_STATE)
    o6 = wout.astype(BF16).reshape(nt, tg, 2, SSM_STATE, ls, SSM_GROUP).transpose(0, 2, 1, 3, 4, 5)
    wout_tile = (o6[:, :, :, :, :, None, :] * eye[None, None, :, None, None, :, None]
                 ).reshape(nt, 2 * tg * SSM_STATE, ls * LANES)
    al_re = p_re[:, ls].reshape(nt, tg * SSM_STATE)
    al_im = p_im[:, ls].reshape(nt, tg * SSM_STATE)
    a_rows = jnp.stack([jnp.concatenate([al_re, al_re], -1), jnp.concatenate([-al_im, al_im], -1)], axis=1)
    return t_tile, win_tile, wout_tile, a_rows


def _ssm_body(n_prompt, nb, t_dec, xn_ref, t_ref, win_ref, wout_ref, a_ref, h0_ref,
              y_ref, hp_ref, hs_ref, u_s, v_s, hin_s):
    ls = SSM_SUB
    rp = n_prompt // ls
    ns = t_dec // ls
    half = v_s.shape[1] // 2

    def sub_rows(ref, s, j):
        if j is None:
            return ref.at[pl.ds(s, rp, stride=ls), :]
        return ref.at[pl.ds(n_prompt + j * ls + s, nb, stride=t_dec), :]

    for s in range(ls):
        cols = slice(s * LANES, (s + 1) * LANES)
        u_s[0:rp, cols] = sub_rows(xn_ref, s, None)[...].astype(BF16)
        for j in range(ns):
            u_s[rp + j * nb:rp + (j + 1) * nb, cols] = sub_rows(xn_ref, s, j)[...].astype(BF16)

    v_s[...] = jnp.dot(u_s[...], win_ref[...], preferred_element_type=F32)
    a1 = a_ref[0:1, :]
    a2 = a_ref[1:2, :]

    def step(h, v):
        h_sw = jnp.concatenate([h[:, half:], h[:, :half]], axis=1)
        return h * a1 + h_sw * a2 + v

    def body(c, h):
        hin_s[pl.ds(c, 1), :] = h
        return step(h, v_s[pl.ds(c, 1), :])

    h = lax.fori_loop(0, rp, body, jnp.zeros((1, 2 * half), F32), unroll=4)
    hp_ref[...] = h
    h = h0_ref[...]
    for j in range(ns):
        r0 = rp + j * nb
        hin_s[r0:r0 + nb, :] = h
        h = step(h, v_s[r0:r0 + nb, :])
    hs_ref[...] = h

    hin = hin_s[...].astype(BF16)
    for n0 in range(0, ls * LANES, MXU_DIM):
        k_hi = n0 + MXU_DIM
        ycol = (jnp.dot(u_s[:, :k_hi], t_ref[:k_hi, n0:n0 + MXU_DIM], preferred_element_type=F32)
                + jnp.dot(hin, wout_ref[:, n0:n0 + MXU_DIM], preferred_element_type=F32))
        for t in range(n0 // LANES, (n0 + MXU_DIM) // LANES):
            piece = ycol[:, t * LANES - n0:(t + 1) * LANES - n0]
            sub_rows(y_ref, t, None)[...] = piece[0:rp]
            for j in range(ns):
                sub_rows(y_ref, t, j)[...] = piece[rp + j * nb:rp + (j + 1) * nb]


def ssm_mix(xn, n_prompt, n_batch, h0_re, h0_im, prep):
    rows, d = xn.shape
    g = d // SSM_GROUP
    ls = SSM_SUB
    tg = SSM_TILE_GROUPS
    nt = g // tg
    t_dec = (rows - n_prompt) // n_batch
    r = n_prompt // ls + (t_dec // ls) * n_batch
    kw = ls * LANES
    sw = 2 * tg * SSM_STATE
    t_tile, win_tile, wout_tile, a_rows = prep
    h0 = jnp.concatenate([h0_re.astype(F32).reshape(n_batch, nt, sw // 2),
                          h0_im.astype(F32).reshape(n_batch, nt, sw // 2)], axis=-1).transpose(1, 0, 2)

    mat = lambda a, b: pl.BlockSpec((None, a, b), lambda i: (i, 0, 0))
    col = pl.BlockSpec((rows, LANES), lambda i: (0, i))
    vmem = (4 * rows * LANES * 4 + 2 * (kw * kw + 2 * kw * sw) * 2 + r * kw * 2 + 2 * r * sw * 4
            + r * sw * 4 + 2 * r * MXU_DIM * 4)
    y, hp, hs = pl.pallas_call(
        functools.partial(_ssm_body, n_prompt, n_batch, t_dec),
        grid=(nt,),
        in_specs=[col, mat(kw, kw), mat(kw, sw), mat(sw, kw), mat(2, sw), mat(n_batch, sw)],
        out_specs=[col, mat(1, sw), mat(n_batch, sw)],
        out_shape=[jax.ShapeDtypeStruct((rows, d), F32), jax.ShapeDtypeStruct((nt, 1, sw), F32),
                   jax.ShapeDtypeStruct((nt, n_batch, sw), F32)],
        scratch_shapes=[pltpu.VMEM((r, kw), BF16), pltpu.VMEM((r, sw), F32), pltpu.VMEM((r, sw), F32)],
        compiler_params=_params(("arbitrary",), vmem),
        name="ssm",
    )(xn, t_tile, win_tile, wout_tile, a_rows, h0)

    half = sw // 2
    re_p = hp[:, 0, :half].reshape(g, SSM_STATE)[None]
    im_p = hp[:, 0, half:].reshape(g, SSM_STATE)[None]
    re_s = hs[:, :, :half].transpose(1, 0, 2).reshape(n_batch, g, SSM_STATE)
    im_s = hs[:, :, half:].transpose(1, 0, 2).reshape(n_batch, g, SSM_STATE)
    return y, re_p, im_p, re_s, im_s


def _gelu_body(y_ref, xn_ref, d_ref, o_ref):
    y = y_ref[...] + d_ref[...] * xn_ref[...]
    c = math.sqrt(2.0 / math.pi)
    cdf = 0.5 * (1.0 + jnp.tanh(c * (y + 0.044715 * (y * y * y))))
    o_ref[...] = (y * cdf).astype(o_ref.dtype)


def skip_gelu(y, xn, d_skip):
    m, d = y.shape
    tm = min(NORM_TILE, m)
    row = pl.BlockSpec((tm, d), lambda i: (i, 0))
    return pl.pallas_call(
        _gelu_body,
        grid=(m // tm,),
        in_specs=[row, row, pl.BlockSpec((1, d), lambda i: (0, 0))],
        out_specs=row,
        out_shape=jax.ShapeDtypeStruct((m, d), BF16),
        compiler_params=_params(("arbitrary",), tm * d * (2 * 10 + 4 * 4)),
        name="skip_gelu",
    )(y, xn, d_skip.reshape(1, d).astype(F32))


_NT = (((1,), (1,)), ((), ()))


def _softmax_rows(s):
    m = jnp.max(s, axis=-1, keepdims=True)
    e = jnp.exp(s - m)
    return e / jnp.sum(e, axis=-1, keepdims=True)


def _memattn_body(kv_per_step, n_heads, q_ref, k_ref, v_ref, *refs):
    o_ref, kb, vb = refs[-3:]

    def load_kv():
        kb[...] = k_ref[...].astype(BF16)
        vb[...] = v_ref[...].astype(BF16)

    if kv_per_step:
        load_kv()
    else:
        pl.when(pl.program_id(0) == 0)(load_kv)
    dh = q_ref.shape[1] // n_heads
    scale = dh ** -0.5
    for h in range(n_heads):
        sl = slice(h * dh, (h + 1) * dh)
        s = lax.dot_general(q_ref[:, sl], kb[:, sl], _NT, preferred_element_type=F32) * scale
        p = _softmax_rows(s).astype(BF16)
        o_ref[:, sl] = jnp.dot(p, vb[:, sl], preferred_element_type=F32).astype(o_ref.dtype)


def _memattn_cache_body(q_ref, k_ref, v_ref, prev_ref, o_ref):
    del prev_ref
    n_heads, dh = k_ref.shape[1], k_ref.shape[2]
    scale = dh ** -0.5
    for h in range(n_heads):
        sl = slice(h * dh, (h + 1) * dh)
        s = lax.dot_general(q_ref[:, sl], k_ref[:, h, :].astype(BF16), _NT, preferred_element_type=F32) * scale
        p = _softmax_rows(s).astype(BF16)
        o_ref[:, sl] = jnp.dot(p, v_ref[:, h, :].astype(BF16), preferred_element_type=F32).astype(o_ref.dtype)


def mem_attention(q, kv_prompt, cache_k, cache_v, layer, n_prompt, n_batch):
    rows, d = q.shape
    mt = kv_prompt.shape[0]
    tm = min(ROW_TILE, n_prompt)
    t_dec = (rows - n_prompt) // n_batch
    scratch = [pltpu.VMEM((mt, d), BF16), pltpu.VMEM((mt, d), BF16)]
    vmem_kv = 2 * 2 * mt * d * 4 + 2 * mt * d * 2
    o = pl.pallas_call(
        functools.partial(_memattn_body, False, MEM_HEADS),
        grid=(n_prompt // tm,),
        in_specs=[pl.BlockSpec((tm, d), lambda i: (i, 0)),
                  pl.BlockSpec((mt, d), lambda i: (0, 0)),
                  pl.BlockSpec((mt, d), lambda i: (0, 1))],
        out_specs=pl.BlockSpec((tm, d), lambda i: (i, 0)),
        out_shape=jax.ShapeDtypeStruct((rows, d), BF16),
        scratch_shapes=scratch,
        compiler_params=_params(("arbitrary",), vmem_kv + 4 * tm * d * 2 + 2 * tm * d * 4),
        name="mem_attn_prompt",
    )(q, kv_prompt, kv_prompt)
    pb = n_prompt // t_dec
    dh = d // MEM_HEADS
    slab = pl.BlockSpec((None, None, mt, MEM_HEADS, dh), lambda b: (layer, b, 0, 0, 0))
    return pl.pallas_call(
        _memattn_cache_body,
        grid=(n_batch,),
        in_specs=[pl.BlockSpec((t_dec, d), lambda b: (pb + b, 0)), slab, slab,
                  pl.BlockSpec(memory_space=pl.ANY)],
        out_specs=pl.BlockSpec((t_dec, d), lambda b: (pb + b, 0)),
        out_shape=jax.ShapeDtypeStruct((rows, d), BF16),
        input_output_aliases={3: 0},
        compiler_params=_params(("arbitrary",), 2 * 2 * 2 * mt * d * 4 + 8 * t_dec * d * 4),
        name="mem_attn_sample",
    )(q, cache_k, cache_v, o)


def _band_prompt_body(ncb, q_ref, k_ref, v_ref, b_ref, o_ref, kb, vb):
    qb = pl.program_id(1)
    s_len, dh = k_ref.shape
    past = PAST_CHUNKS * CHUNK

    @pl.when(qb == 0)
    def _():
        for ref, src in ((kb, k_ref), (vb, v_ref)):
            ref[0:past, :] = jnp.zeros((past, dh), BF16)
            ref[past + s_len:past + s_len + CHUNK, :] = jnp.zeros((CHUNK, dh), BF16)
            ref[past:past + s_len, :] = src[...].astype(BF16)

    bias = b_ref[...]
    col = lax.broadcasted_iota(jnp.int32, (CHUNK, BAND_WIN), 1)
    scale = dh ** -0.5

    def chunk(ci, carry):
        c = qb * ncb + ci
        r0 = pl.multiple_of(ci * CHUNK, CHUNK)
        w0 = pl.multiple_of(c * CHUNK, CHUNK)
        q = q_ref[pl.ds(r0, CHUNK), :]
        s = lax.dot_general(q, kb[pl.ds(w0, BAND_WIN), :], _NT, preferred_element_type=F32) * scale + bias
        s = jnp.where(col >= (PAST_CHUNKS - c) * CHUNK, s, NEG)
        p = _softmax_rows(s).astype(BF16)
        o_ref[pl.ds(r0, CHUNK), :] = jnp.dot(p, vb[pl.ds(w0, BAND_WIN), :],
                                             preferred_element_type=F32).astype(o_ref.dtype)
        return carry

    lax.fori_loop(0, ncb, chunk, 0, unroll=BAND_UNROLL)


def _band_sample_body(hb, dh, q_ref, kc_ref, vc_ref, kn_ref, vn_ref, bc_ref, bn_ref, prev_ref, o_ref):
    del prev_ref
    scale = dh ** -0.5
    for h in range(hb):
        sl = slice(h * dh, (h + 1) * dh)
        q = q_ref[:, sl]
        s1 = lax.dot_general(q, kc_ref[:, h, :].astype(BF16), _NT, preferred_element_type=F32) * scale + bc_ref[h]
        s2 = lax.dot_general(q, kn_ref[:, sl].astype(BF16), _NT, preferred_element_type=F32) * scale + bn_ref[h]
        m = jnp.maximum(jnp.max(s1, axis=-1, keepdims=True), jnp.max(s2, axis=-1, keepdims=True))
        e1 = jnp.exp(s1 - m)
        e2 = jnp.exp(s2 - m)
        den = jnp.sum(e1, axis=-1, keepdims=True) + jnp.sum(e2, axis=-1, keepdims=True)
        o = (jnp.dot((e1 / den).astype(BF16), vc_ref[:, h, :].astype(BF16), preferred_element_type=F32)
             + jnp.dot((e2 / den).astype(BF16), vn_ref[:, sl].astype(BF16), preferred_element_type=F32))
        o_ref[:, sl] = o.astype(o_ref.dtype)


def band_attention(q, kv, cache_k, cache_v, table, n_prompt, n_batch):
    rows, d = q.shape
    nh = N_HEADS
    dh = d // nh
    past = PAST_CHUNKS * CHUNK
    band = past + CHUNK
    t_dec = (rows - n_prompt) // n_batch
    cb = cache_k.shape[1]

    rel = past + jnp.arange(CHUNK)[:, None] - jnp.arange(band)[None, :]
    bias = table[:, jnp.clip(rel, -MAX_REL, MAX_REL) + MAX_REL].astype(F32)
    bias_p = jnp.pad(bias, ((0, 0), (0, 0), (0, BAND_WIN - band)), constant_values=NEG)
    rel_s = cb + jnp.arange(t_dec)[:, None] - jnp.arange(cb + t_dec)[None, :]
    bias_s = table[:, jnp.clip(rel_s, -MAX_REL, MAX_REL) + MAX_REL].astype(F32)
    bias_c, bias_n = bias_s[:, :, :cb], bias_s[:, :, cb:]

    bq = min(BAND_Q_ROWS, n_prompt)
    ncb = bq // CHUNK
    pad_rows = past + n_prompt + CHUNK
    o = pl.pallas_call(
        functools.partial(_band_prompt_body, ncb),
        grid=(nh, n_prompt // bq),
        in_specs=[pl.BlockSpec((bq, dh), lambda h, i: (i, h)),
                  pl.BlockSpec((n_prompt, dh), lambda h, i: (0, h)),
                  pl.BlockSpec((n_prompt, dh), lambda h, i: (0, nh + h)),
                  pl.BlockSpec((None, CHUNK, BAND_WIN), lambda h, i: (h, 0, 0))],
        out_specs=pl.BlockSpec((bq, dh), lambda h, i: (i, h)),
        out_shape=jax.ShapeDtypeStruct((rows, d), BF16),
        scratch_shapes=[pltpu.VMEM((pad_rows, dh), BF16), pltpu.VMEM((pad_rows, dh), BF16)],
        compiler_params=_params(("arbitrary", "arbitrary"),
                                4 * n_prompt * dh * 4 + 2 * pad_rows * dh * 2 + 8 * bq * dh * 2),
        name="band_attn_prompt",
    )(q, kv, kv, bias_p)

    hb = min(SAMPLE_HEADS_PER_STEP, nh)
    ngrp = nh // hb
    pb = n_prompt // t_dec
    w = hb * dh
    return pl.pallas_call(
        functools.partial(_band_sample_body, hb, dh),
        grid=(n_batch, ngrp),
        in_specs=[pl.BlockSpec((t_dec, w), lambda b, g: (pb + b, g)),
                  pl.BlockSpec((None, cb, hb, dh), lambda b, g: (b, 0, g, 0)),
                  pl.BlockSpec((None, cb, hb, dh), lambda b, g: (b, 0, g, 0)),
                  pl.BlockSpec((t_dec, w), lambda b, g: (pb + b, g)),
                  pl.BlockSpec((t_dec, w), lambda b, g: (pb + b, ngrp + g)),
                  pl.BlockSpec((hb, t_dec, cb), lambda b, g: (g, 0, 0)),
                  pl.BlockSpec((hb, t_dec, t_dec), lambda b, g: (g, 0, 0)),
                  pl.BlockSpec(memory_space=pl.ANY)],
        out_specs=pl.BlockSpec((t_dec, w), lambda b, g: (pb + b, g)),
        out_shape=jax.ShapeDtypeStruct((rows, d), BF16),
        input_output_aliases={7: 0},
        compiler_params=_params(("arbitrary", "arbitrary"), 4 * cb * w * 4 + 16 * t_dec * w * 4 + 4 * hb * t_dec * cb * 4),
        name="band_attn_sample",
    )(q, cache_k, cache_v, kv, kv, bias_c, bias_n, o)


def _gather_rows(src_hbm, idx_ref, dst_ref, n_rows, sem):
    def row_copy(r, src_row):
        return pltpu.make_async_copy(src_hbm.at[pl.ds(src_row, 1), :], dst_ref.at[pl.ds(r, 1), :], sem)

    def issue(r, carry):
        row_copy(r, idx_ref[0, r]).start()
        return carry

    def drain(r, carry):
        row_copy(r, 0).wait()
        return carry

    lax.fori_loop(0, n_rows, issue, 0)
    lax.fori_loop(0, n_rows, drain, 0)


def _moe_body(item_e, item_blk, item_nsub, slot_ref, xn_hbm, wg_ref, wu_ref, wo_ref, o_ref,
              xbuf, wgb, wub, wob, sem):
    del item_e, item_blk
    i = pl.program_id(0)
    j = pl.program_id(1)
    nsub = item_nsub[i]

    @pl.when(nsub > 0)
    def _():
        @pl.when(j == 0)
        def _():
            _gather_rows(xn_hbm, slot_ref, xbuf, nsub * MOE_SUB, sem)
            o_ref[...] = jnp.zeros_like(o_ref)

        wgb[...] = wg_ref[...].astype(BF16)
        wub[...] = wu_ref[...].astype(BF16)
        wob[...] = wo_ref[...].astype(BF16)

        def sub(s, carry):
            r0 = pl.multiple_of(s * MOE_SUB, MOE_SUB)
            xb = xbuf[pl.ds(r0, MOE_SUB), :].astype(BF16)
            g = jnp.dot(xb, wgb[...], preferred_element_type=F32)
            u = jnp.dot(xb, wub[...], preferred_element_type=F32)
            h = (g * jax.nn.sigmoid(g) * u).astype(BF16)
            o_ref[pl.ds(r0, MOE_SUB), :] += jnp.dot(h, wob[...], preferred_element_type=F32)
            return carry

        lax.fori_loop(0, nsub, sub, 0)


def moe_experts(xn, slot_tok, w_in, w_out, layer, item_e, item_blk, item_nsub):
    d = xn.shape[1]
    n_items = slot_tok.shape[0]
    n_slots = n_items * MOE_ROWS
    de = w_out.shape[2]
    tj = min(MOE_COL, de)
    nj = de // tj

    def jj(i, j, ie, ib, ns):
        return jnp.where(ns[i] > 0, j, nj - 1)

    grid_spec = pltpu.PrefetchScalarGridSpec(
        num_scalar_prefetch=3,
        grid=(n_items, nj),
        in_specs=[pl.BlockSpec((None, 1, MOE_ROWS), lambda i, j, ie, ib, ns: (ib[i], 0, 0),
                               memory_space=pltpu.SMEM),
                  pl.BlockSpec(memory_space=pl.ANY),
                  pl.BlockSpec((None, None, d, tj), lambda i, j, ie, ib, ns: (layer, ie[i], 0, jj(i, j, ie, ib, ns))),
                  pl.BlockSpec((None, None, d, tj),
                               lambda i, j, ie, ib, ns: (layer, ie[i], 0, nj + jj(i, j, ie, ib, ns))),
                  pl.BlockSpec((None, None, tj, d), lambda i, j, ie, ib, ns: (layer, ie[i], jj(i, j, ie, ib, ns), 0))],
        out_specs=pl.BlockSpec((MOE_ROWS, d), lambda i, j, ie, ib, ns: (ib[i], 0)),
        scratch_shapes=[pltpu.VMEM((MOE_ROWS, d), F32), pltpu.VMEM((d, tj), BF16), pltpu.VMEM((d, tj), BF16),
                        pltpu.VMEM((tj, d), BF16), pltpu.SemaphoreType.DMA(())],
    )
    vmem = 3 * MOE_ROWS * d * 4 + 3 * d * tj * (2 * 4 + 2) + 4 * MOE_SUB * d * 4
    return pl.pallas_call(
        _moe_body,
        grid_spec=grid_spec,
        out_shape=jax.ShapeDtypeStruct((n_slots, d), F32),
        compiler_params=_params(("arbitrary", "arbitrary"), vmem),
        name="moe_experts",
    )(item_e, item_blk, item_nsub, slot_tok, xn, w_in, w_in, w_out)


def _route(logits):
    t = logits.shape[0]
    lg = logits[:, :N_GROUPS]
    g_idx = jnp.argmax(lg, axis=-1).astype(jnp.int32)[:, None]
    in_group = lax.broadcasted_iota(jnp.int32, (t, N_GROUPS), 1) == g_idx
    p_group = jnp.sum(jnp.where(in_group, jax.nn.softmax(lg, axis=-1), 0.0), axis=-1, keepdims=True)
    le = logits[:, N_GROUPS:N_GROUPS + N_EXPERTS].reshape(t, N_GROUPS, EXPERTS_PER_GROUP)
    le = jnp.sum(jnp.where(in_group[:, :, None], le, 0.0), axis=1)
    lane = lax.broadcasted_iota(jnp.int32, (t, EXPERTS_PER_GROUP), 1)
    tops_v, tops_j = [], []
    for _ in range(TOP_K):
        j = jnp.argmax(le, axis=-1).astype(jnp.int32)[:, None]
        tops_j.append(j)
        tops_v.append(jnp.max(le, axis=-1, keepdims=True))
        le = jnp.where(lane == j, -jnp.inf, le)
    top_v = jnp.concatenate(tops_v, axis=-1)
    top_j = jnp.concatenate(tops_j, axis=-1)
    gate = p_group * jax.nn.softmax(top_v, axis=-1)
    eidx = g_idx * EXPERTS_PER_GROUP + top_j
    return eidx.astype(jnp.int32), gate


def _dispatch(eidx):
    t = eidx.shape[0]
    a = t * TOP_K
    n_items = a // MOE_ROWS + N_EXPERTS
    flat_e = eidx.reshape(a)
    flat_tok = jnp.repeat(jnp.arange(t, dtype=jnp.int32), TOP_K)
    order = jnp.argsort(flat_e)
    se = flat_e[order]
    counts = jnp.zeros((N_EXPERTS,), jnp.int32).at[flat_e].add(1)
    nblk = (counts + MOE_ROWS - 1) // MOE_ROWS
    blk_end = jnp.cumsum(nblk)
    blk_start = blk_end - nblk
    start = jnp.cumsum(counts) - counts
    rank = jnp.arange(a, dtype=jnp.int32) - start[se]
    dest_sorted = blk_start[se] * MOE_ROWS + rank
    slot_tok = jnp.zeros((n_items * MOE_ROWS,), jnp.int32).at[dest_sorted].set(flat_tok[order])
    dest = jnp.zeros((a,), jnp.int32).at[order].set(dest_sorted).reshape(t, TOP_K)
    n_used = blk_end[-1]
    item = jnp.arange(n_items, dtype=jnp.int32)
    item_c = jnp.minimum(item, n_used - 1)
    item_e = jnp.minimum(jnp.searchsorted(blk_end, item_c, side='right'), N_EXPERTS - 1).astype(jnp.int32)
    rows_left = counts[item_e] - (item_c - blk_start[item_e]) * MOE_ROWS
    nsub = (jnp.clip(rows_left, 0, MOE_ROWS) + MOE_SUB - 1) // MOE_SUB
    item_nsub = jnp.where(item < n_used, nsub, 0).astype(jnp.int32)
    return slot_tok.reshape(n_items, 1, MOE_ROWS), dest, item_e, item_c.astype(jnp.int32), item_nsub


def _combine_body(n_blocks, d_cur, d_next, gate_ref, x_ref, ys_hbm, o_ref, buf, sem):
    i = pl.program_id(0)
    slot = i % 2
    tb = x_ref.shape[0]

    def row_copy(s, k, r, src_row):
        return pltpu.make_async_copy(ys_hbm.at[pl.ds(src_row, 1), :], buf.at[s, k, pl.ds(r, 1), :], sem.at[s])

    def issue(idx_ref, s):
        def body(r, carry):
            for k in range(TOP_K):
                row_copy(s, k, r, idx_ref[k, r]).start()
            return carry
        lax.fori_loop(0, tb, body, 0)

    @pl.when(i == 0)
    def _():
        issue(d_cur, 0)

    @pl.when(i + 1 < n_blocks)
    def _():
        issue(d_next, 1 - slot)

    def drain(r, carry):
        for k in range(TOP_K):
            row_copy(slot, k, r, 0).wait()
        return carry

    lax.fori_loop(0, tb, drain, 0)
    acc = buf[slot, 0] * gate_ref[:, 0:1]
    for k in range(1, TOP_K):
        acc = acc + buf[slot, k] * gate_ref[:, k:k + 1]
    o_ref[...] = x_ref[...] + acc


def moe_combine(x, ys, dest, gate):
    t, d = x.shape
    tb = min(COMBINE_ROWS, t)
    n_blocks = t // tb
    dest3 = dest.reshape(n_blocks, tb, TOP_K).transpose(0, 2, 1)
    idx = lambda f: pl.BlockSpec((None, TOP_K, tb), lambda i: (f(i), 0, 0), memory_space=pltpu.SMEM)
    row = pl.BlockSpec((tb, d), lambda i: (i, 0))
    return pl.pallas_call(
        functools.partial(_combine_body, n_blocks),
        grid=(n_blocks,),
        in_specs=[idx(lambda i: i), idx(lambda i: jnp.minimum(i + 1, n_blocks - 1)),
                  pl.BlockSpec((tb, TOP_K), lambda i: (i, 0)), row, pl.BlockSpec(memory_space=pl.ANY)],
        out_specs=row,
        out_shape=jax.ShapeDtypeStruct((t, d), F32),
        scratch_shapes=[pltpu.VMEM((2, TOP_K, tb, d), F32), pltpu.SemaphoreType.DMA((2,))],
        compiler_params=_params(("arbitrary",), (2 * TOP_K + 6) * tb * d * 4),
        name="moe_combine",
    )(dest3, dest3, gate, x, ys)


def moe_ffn(x, xn, logits, w_in, w_out, layer):
    eidx, gate = _route(logits)
    slot_tok, dest, item_e, item_blk, item_nsub = _dispatch(eidx)
    ys = moe_experts(xn, slot_tok, w_in, w_out, layer, item_e, item_blk, item_nsub)
    return moe_combine(x, ys, dest, gate)


def _router_weights(w_group, b_group, w_expert, b_expert):
    d = w_group.shape[0]
    pad = LANES - N_GROUPS - N_EXPERTS
    w = jnp.concatenate([w_group, w_expert, jnp.zeros((d, pad), w_group.dtype)], axis=1).astype(BF16)
    b = jnp.concatenate([b_group.astype(F32), b_expert.astype(F32), jnp.zeros((pad,), F32)]).reshape(1, LANES)
    return w, b


def kernel(x_prompt, x_sample, state_ssm_re, state_ssm_im, cache_band_k, cache_band_v, cache_mem_k, cache_mem_v, mem_prompt, norm_mix, norm_mem, norm_memin, norm_ffn, norm_kv, norm_final, ssm_a_re, ssm_a_im, ssm_log_dt, ssm_b_re, ssm_b_im, ssm_c_re, ssm_c_im, ssm_d, ssm_w_glu, w_kv_shared, attn_w_q, attn_rel_bias, attn_w_o, mem_w_q, mem_w_kv, mem_w_o, moe_w_group, moe_b_group, moe_w_expert, moe_b_expert, moe_w_in, moe_w_out):
    bp, seq, d = x_prompt.shape
    nb, t_dec, _ = x_sample.shape
    assert bp == 1
    n_prompt = bp * seq
    depth = norm_mix.shape[0]
    n_a = ssm_a_re.shape[0]
    dh = d // N_HEADS
    mt = mem_prompt.shape[1]
    x = jnp.concatenate([x_prompt.reshape(n_prompt, d), x_sample.reshape(nb * t_dec, d)], axis=0)
    mem = mem_prompt.reshape(mt, d)

    out_re_p, out_im_p, out_re_s, out_im_s, out_mk, out_mv = [], [], [], [], [], []
    kv = None
    for layer in range(depth):
        if layer < n_a:
            a = layer
            (xn,) = rmsnorm(x, [norm_mix[layer]], [F32])
            prep = _ssm_prep(ssm_a_re[a], ssm_a_im[a], ssm_log_dt[a], ssm_b_re[a], ssm_b_im[a],
                             ssm_c_re[a], ssm_c_im[a])
            y, re_p, im_p, re_s, im_s = ssm_mix(xn, n_prompt, nb, state_ssm_re[a], state_ssm_im[a], prep)
            out_re_p.append(re_p)
            out_im_p.append(im_p)
            out_re_s.append(re_s)
            out_im_s.append(im_s)
            g = skip_gelu(y, xn, ssm_d[a])
            x = matmul(g, ssm_w_glu, prefix=(a,), n_out=d, glu=True, res=x, name="glu_proj")
        else:
            b = layer - n_a
            if layer == n_a:
                kvn, xn = rmsnorm(x, [norm_kv, norm_mix[layer]], [BF16, BF16])
                kv = matmul(kvn, w_kv_shared, name="kv_proj")
            else:
                (xn,) = rmsnorm(x, [norm_mix[layer]], [BF16])
            q = matmul(xn, attn_w_q, prefix=(b,), out_dtype=BF16, name="attn_q_proj")
            o = band_attention(q, kv, cache_band_k, cache_band_v, attn_rel_bias[b], n_prompt, nb)
            x = matmul(o, attn_w_o, prefix=(b,), res=x, name="attn_o_proj")

        (memn,) = rmsnorm(mem, [norm_memin[layer]], [BF16])
        kv_mem = matmul(memn, mem_w_kv, prefix=(layer,), name="mem_kv_proj")
        out_mk.append(kv_mem[:, :d].reshape(bp, mt, MEM_HEADS, d // MEM_HEADS))
        out_mv.append(kv_mem[:, d:].reshape(bp, mt, MEM_HEADS, d // MEM_HEADS))
        (xn,) = rmsnorm(x, [norm_mem[layer]], [BF16])
        q = matmul(xn, mem_w_q, prefix=(layer,), out_dtype=BF16, name="mem_q_proj")
        o = mem_attention(q, kv_mem, cache_mem_k, cache_mem_v, layer, n_prompt, nb)
        x = matmul(o, mem_w_o, prefix=(layer,), res=x, name="mem_o_proj")

        w_r, b_r = _router_weights(moe_w_group[layer], moe_b_group[layer], moe_w_expert[layer], moe_b_expert[layer])
        xn, logits = rmsnorm_router(x, norm_ffn[layer], w_r, b_r)
        x = moe_ffn(x, xn, logits, moe_w_in, moe_w_out, layer)

    y_p, y_s = rmsnorm_final(x, norm_final, n_prompt)
    keep = min(PAST_CHUNKS * CHUNK, seq)
    k_all, v_all = kv[:, :d], kv[:, d:]
    bk_p = k_all[n_prompt - keep:n_prompt].reshape(bp, keep, N_HEADS, dh)
    bv_p = v_all[n_prompt - keep:n_prompt].reshape(bp, keep, N_HEADS, dh)
    keep_s = min(PAST_CHUNKS * CHUNK, t_dec)
    bk_s = k_all[n_prompt:].reshape(nb, t_dec, N_HEADS, dh)[:, t_dec - keep_s:]
    bv_s = v_all[n_prompt:].reshape(nb, t_dec, N_HEADS, dh)[:, t_dec - keep_s:]
    return (y_p.reshape(bp, seq, d), y_s.reshape(nb, t_dec, d),
            jnp.stack(out_re_p), jnp.stack(out_im_p), jnp.stack(out_re_s), jnp.stack(out_im_s),
            bk_p, bv_p, bk_s, bv_s, jnp.stack(out_mk), jnp.stack(out_mv))
```

```python
import functools
import math

import jax
import jax.numpy as jnp
from jax import lax
from jax.experimental import pallas as pl
from jax.experimental.pallas import tpu as pltpu

F32 = jnp.float32
BF16 = jnp.bfloat16

CHUNK = 64
RMS_EPS = 1e-6
SSM_GROUP = 16
SSM_STATE = 64
N_HEADS = 32
PAST_CHUNKS = 8
MAX_REL = 256
MEM_HEADS = 4
N_GROUPS = 8
EXPERTS_PER_GROUP = 8
N_EXPERTS = N_GROUPS * EXPERTS_PER_GROUP
TOP_K = 2

V7X_VMEM_BYTES = 64 * 1024 * 1024
LANES = 128
MXU_DIM = 256

BF16_ROWS = 16
ROW_TILE = 512
MATMUL_ROWS_MAX = 1152
NORM_TILE = 256
COL_TILE = 512
SSM_SUB = 8
SSM_TILE_GROUPS = LANES // SSM_GROUP
BAND_Q_ROWS = 1024
BAND_GROUP = 4
BAND_ROWS = BAND_GROUP * CHUNK
BAND_WIN = (PAST_CHUNKS + BAND_GROUP) * CHUNK
SAMPLE_HEADS_PER_STEP = 8
MOE_ROWS = 384
MOE_SUB = 128
MOE_COL = 256
COMBINE_ROWS = 128
GATHER_UNROLL = 8
NEG = -1e30


def _params(sem, vmem_bytes):
    limit = min(int(vmem_bytes) + (4 << 20), V7X_VMEM_BYTES - (8 << 20))
    return pltpu.CompilerParams(dimension_semantics=sem, vmem_limit_bytes=limit)


def _rms_body(n_out, x_ref, *refs):
    g_refs, o_refs = refs[:n_out], refs[n_out:]
    x = x_ref[...]
    y = x * lax.rsqrt(jnp.mean(x * x, axis=-1, keepdims=True) + RMS_EPS)
    for g_ref, o_ref in zip(g_refs, o_refs):
        o_ref[...] = (y * g_ref[...]).astype(o_ref.dtype)


def rmsnorm(x, gains, dtypes):
    m, d = x.shape
    tm = min(NORM_TILE, m)
    n = len(gains)
    row = pl.BlockSpec((tm, d), lambda i: (i, 0))
    vec = pl.BlockSpec((1, d), lambda i: (0, 0))
    outs = pl.pallas_call(
        functools.partial(_rms_body, n),
        grid=(m // tm,),
        in_specs=[row] + [vec] * n,
        out_specs=[row] * n,
        out_shape=[jax.ShapeDtypeStruct((m, d), dt) for dt in dtypes],
        compiler_params=_params(("arbitrary",), tm * d * 4 * (2 * (1 + n) + 3)),
        name="rmsnorm",
    )(x, *[g.reshape(1, d).astype(F32) for g in gains])
    return outs


def _rms_final_body(x_ref, g_ref, op_ref, os_ref, *, n_prompt_blocks):
    x = x_ref[...]
    y = x * lax.rsqrt(jnp.mean(x * x, axis=-1, keepdims=True) + RMS_EPS) * g_ref[...]
    i = pl.program_id(0)

    @pl.when(i < n_prompt_blocks)
    def _():
        op_ref[...] = y

    @pl.when(i >= n_prompt_blocks)
    def _():
        os_ref[...] = y


def rmsnorm_final(x, gain, n_prompt):
    m, d = x.shape
    tm = math.gcd(NORM_TILE, math.gcd(n_prompt, m - n_prompt))
    npb = n_prompt // tm
    return pl.pallas_call(
        functools.partial(_rms_final_body, n_prompt_blocks=npb),
        grid=(m // tm,),
        in_specs=[pl.BlockSpec((tm, d), lambda i: (i, 0)), pl.BlockSpec((1, d), lambda i: (0, 0))],
        out_specs=[pl.BlockSpec((tm, d), lambda i: (jnp.minimum(i, npb - 1), 0)),
                   pl.BlockSpec((tm, d), lambda i: (jnp.maximum(i - npb, 0), 0))],
        out_shape=[jax.ShapeDtypeStruct((n_prompt, d), F32), jax.ShapeDtypeStruct((m - n_prompt, d), F32)],
        compiler_params=_params(("arbitrary",), 9 * tm * d * 4),
        name="rmsnorm_final",
    )(x, gain.reshape(1, d).astype(F32))


def _rms_router_body(x_ref, g_ref, wr_ref, br_ref, xn_ref, lg_ref):
    x = x_ref[...]
    y = x * lax.rsqrt(jnp.mean(x * x, axis=-1, keepdims=True) + RMS_EPS)
    xn = y * g_ref[...]
    xn_ref[...] = xn
    lg_ref[...] = jnp.dot(xn.astype(BF16), wr_ref[...], preferred_element_type=F32) + br_ref[...]


def rmsnorm_router(x, gain, w_router, b_router):
    m, d = x.shape
    tm = min(NORM_TILE, m)
    nr = w_router.shape[1]
    return pl.pallas_call(
        _rms_router_body,
        grid=(m // tm,),
        in_specs=[pl.BlockSpec((tm, d), lambda i: (i, 0)), pl.BlockSpec((1, d), lambda i: (0, 0)),
                  pl.BlockSpec((d, nr), lambda i: (0, 0)), pl.BlockSpec((1, nr), lambda i: (0, 0))],
        out_specs=[pl.BlockSpec((tm, d), lambda i: (i, 0)), pl.BlockSpec((tm, nr), lambda i: (i, 0))],
        out_shape=[jax.ShapeDtypeStruct((m, d), F32), jax.ShapeDtypeStruct((m, nr), F32)],
        compiler_params=_params(("arbitrary",), tm * d * (2 * 8 + 3 * 4) + 4 * d * nr),
        name="rmsnorm_router",
    )(x, gain.reshape(1, d).astype(F32), w_router, b_router)


def _mm_body(glu, has_res, x_ref, *refs):
    nw = 2 if glu else 1
    w_refs = refs[:nw]
    res_ref = refs[nw] if has_res else None
    o_ref = refs[nw + (1 if has_res else 0)]
    wb_refs = refs[nw + (1 if has_res else 0) + 1:]

    @pl.when(pl.program_id(1) == 0)
    def _():
        for w_ref, wb_ref in zip(w_refs, wb_refs):
            wb_ref[...] = w_ref[...].astype(BF16)

    x = x_ref[...]
    acc = jnp.dot(x, wb_refs[0][...], preferred_element_type=F32)
    if glu:
        gate = jnp.dot(x, wb_refs[1][...], preferred_element_type=F32)
        acc = acc * jax.nn.sigmoid(gate)
    if has_res:
        acc = res_ref[...] + acc
    o_ref[...] = acc.astype(o_ref.dtype)


def _matmul_rows(m):
    best = None
    for t in range(BF16_ROWS, min(m, MATMUL_ROWS_MAX) + 1, BF16_ROWS):
        if m % t == 0:
            best = t
    return best if best is not None else m


def matmul(x, w, prefix=(), n_out=None, glu=False, res=None, out_dtype=F32, name="matmul"):
    m, k = x.shape
    n = n_out if n_out is not None else w.shape[-1]
    tm = _matmul_rows(m)
    tn = min(COL_TILE // 2 if glu else COL_TILE, n)
    lead = (None,) * len(prefix)
    nblk = n // tn
    w_specs = [pl.BlockSpec(lead + (k, tn), lambda j, i: prefix + (0, j))]
    if glu:
        w_specs.append(pl.BlockSpec(lead + (k, tn), lambda j, i: prefix + (0, nblk + j)))
    nw = len(w_specs)
    in_specs = [pl.BlockSpec((tm, k), lambda j, i: (i, 0))] + w_specs
    args = [x] + [w] * nw
    if res is not None:
        in_specs.append(pl.BlockSpec((tm, tn), lambda j, i: (i, j)))
        args.append(res)
    osize = jnp.dtype(out_dtype).itemsize
    vmem = 2 * tm * k * 2 + nw * (2 * k * tn * 4 + k * tn * 2) + 2 * tm * tn * (osize + 4) + 4 * tm * tn * 4
    return pl.pallas_call(
        functools.partial(_mm_body, glu, res is not None),
        grid=(nblk, m // tm),
        in_specs=in_specs,
        out_specs=pl.BlockSpec((tm, tn), lambda j, i: (i, j)),
        out_shape=jax.ShapeDtypeStruct((m, n), out_dtype),
        scratch_shapes=[pltpu.VMEM((k, tn), BF16)] * nw,
        compiler_params=_params(("arbitrary", "arbitrary"), vmem),
        name=name,
    )(*args)


def _ssm_prep(a_re, a_im, log_dt, b_re, b_im, c_re, c_im):
    hi = lax.Precision.HIGHEST
    ls = SSM_SUB
    a_re = a_re.astype(F32)
    a_im = a_im.astype(F32)
    dt = jnp.exp(log_dt.astype(F32))[:, None]
    mag = jnp.exp(a_re * dt)
    ab_re = mag * jnp.cos(a_im * dt)
    ab_im = mag * jnp.sin(a_im * dt)
    den = a_re * a_re + a_im * a_im
    zr = ab_re - 1.0
    f_re = (zr * a_re + ab_im * a_im) / den
    f_im = (ab_im * a_re - zr * a_im) / den
    b_re = b_re.astype(F32)
    b_im = b_im.astype(F32)
    bb_re = f_re[..., None] * b_re - f_im[..., None] * b_im
    bb_im = f_re[..., None] * b_im + f_im[..., None] * b_re
    c_re = c_re.astype(F32)
    c_im = c_im.astype(F32)

    pw_re = [jnp.ones_like(ab_re)]
    pw_im = [jnp.zeros_like(ab_im)]
    for _ in range(ls):
        pr, pi = pw_re[-1], pw_im[-1]
        pw_re.append(pr * ab_re - pi * ab_im)
        pw_im.append(pr * ab_im + pi * ab_re)
    p_re = jnp.stack(pw_re, axis=1)
    p_im = jnp.stack(pw_im, axis=1)
    g = a_re.shape[0]

    cp_re = c_re[:, None] * p_re[:, :, None, :] - c_im[:, None] * p_im[:, :, None, :]
    cp_im = c_re[:, None] * p_im[:, :, None, :] + c_im[:, None] * p_re[:, :, None, :]
    kern = (jnp.einsum('gtpn,gnq->gtpq', cp_re[:, :ls], bb_re, precision=hi)
            - jnp.einsum('gtpn,gnq->gtpq', cp_im[:, :ls], bb_im, precision=hi))
    s_idx = jnp.arange(ls)[:, None]
    t_idx = jnp.arange(ls)[None, :]
    lag = t_idx - s_idx
    toep = kern[:, jnp.clip(lag, 0, ls - 1)]
    toep = jnp.where((lag >= 0)[None, :, :, None, None], toep, 0.0)
    toep = toep.transpose(0, 1, 4, 2, 3).reshape(g, ls * SSM_GROUP, ls * SSM_GROUP)

    pr_rev = p_re[:, ls - 1::-1][:, :, None, :]
    pi_rev = p_im[:, ls - 1::-1][:, :, None, :]
    bbt_re = bb_re.transpose(0, 2, 1)[:, None]
    bbt_im = bb_im.transpose(0, 2, 1)[:, None]
    win_re = pr_rev * bbt_re - pi_rev * bbt_im
    win_im = pr_rev * bbt_im + pi_rev * bbt_re
    win = jnp.concatenate([win_re, win_im], axis=-1).reshape(g, ls * SSM_GROUP, 2 * SSM_STATE)

    wo_re = cp_re[:, 1:].transpose(0, 3, 1, 2).reshape(g, SSM_STATE, ls * SSM_GROUP)
    wo_im = -cp_im[:, 1:].transpose(0, 3, 1, 2).reshape(g, SSM_STATE, ls * SSM_GROUP)
    wout = jnp.concatenate([wo_re, wo_im], axis=1)

    tg = SSM_TILE_GROUPS
    nt = g // tg
    lp = ls * SSM_GROUP
    c_t = toep.astype(BF16).reshape(nt, tg, ls, SSM_GROUP, lp).transpose(0, 2, 1, 3, 4).reshape(nt, ls * LANES, lp)
    c_win = win.astype(BF16).reshape(nt, tg, ls, SSM_GROUP, 2 * SSM_STATE).transpose(0, 2, 1, 3, 4).reshape(
        nt, ls * LANES, 2 * SSM_STATE)
    c_out = wout.astype(BF16).reshape(nt, tg, 2, SSM_STATE, lp).transpose(0, 2, 1, 3, 4).reshape(
        nt, 2 * tg * SSM_STATE, lp)
    al_re = p_re[:, ls].reshape(nt, tg * SSM_STATE)
    al_im = p_im[:, ls].reshape(nt, tg * SSM_STATE)
    a_rows = jnp.stack([jnp.concatenate([al_re, al_re], -1), jnp.concatenate([-al_im, al_im], -1)], axis=1)
    return c_t, c_win, c_out, a_rows


def _spread_groups(c, row_unit, col_unit):
    tg = SSM_TILE_GROUPS
    r, w = c.shape
    wide = w * tg
    src = lax.broadcasted_iota(jnp.int32, (w, wide), 0)
    dst = lax.broadcasted_iota(jnp.int32, (w, wide), 1)
    sel = ((src // col_unit == dst // (tg * col_unit)) & (src % col_unit == dst % col_unit)).astype(BF16)
    rep = jnp.dot(c, sel, preferred_element_type=F32)
    row_g = (lax.broadcasted_iota(jnp.int32, (r, wide), 0) // row_unit) % tg
    col_g = (lax.broadcasted_iota(jnp.int32, (r, wide), 1) // col_unit) % tg
    return jnp.where(row_g == col_g, rep, 0.0).astype(BF16)


def _ssm_body(n_prompt, nb, t_dec, xn_ref, ct_ref, cwin_ref, cout_ref, a_ref, h0_ref,
              y_ref, hp_ref, hs_ref, u_s, v_s, hin_s, t_ref, win_ref, wout_ref):
    ls = SSM_SUB
    rp = n_prompt // ls
    ns = t_dec // ls
    half = v_s.shape[1] // 2
    t_ref[...] = _spread_groups(ct_ref[...], SSM_GROUP, SSM_GROUP)
    win_ref[...] = _spread_groups(cwin_ref[...], SSM_GROUP, SSM_STATE)
    wout_ref[...] = _spread_groups(cout_ref[...], SSM_STATE, SSM_GROUP)

    def sub_rows(ref, s, j):
        if j is None:
            return ref.at[pl.ds(s, rp, stride=ls), :]
        return ref.at[pl.ds(n_prompt + j * ls + s, nb, stride=t_dec), :]

    for s in range(ls):
        cols = slice(s * LANES, (s + 1) * LANES)
        u_s[0:rp, cols] = sub_rows(xn_ref, s, None)[...].astype(BF16)
        for j in range(ns):
            u_s[rp + j * nb:rp + (j + 1) * nb, cols] = sub_rows(xn_ref, s, j)[...].astype(BF16)

    v_s[...] = jnp.dot(u_s[...], win_ref[...], preferred_element_type=F32)
    a1 = a_ref[0:1, :]
    a2 = a_ref[1:2, :]

    def step(h, v):
        h_sw = jnp.concatenate([h[:, half:], h[:, :half]], axis=1)
        return h * a1 + h_sw * a2 + v

    def body(c, h):
        hin_s[pl.ds(c, 1), :] = h
        return step(h, v_s[pl.ds(c, 1), :])

    h = lax.fori_loop(0, rp, body, jnp.zeros((1, 2 * half), F32), unroll=4)
    hp_ref[...] = h
    h = h0_ref[...]
    for j in range(ns):
        r0 = rp + j * nb
        hin_s[r0:r0 + nb, :] = h
        h = step(h, v_s[r0:r0 + nb, :])
    hs_ref[...] = h

    hin = hin_s[...].astype(BF16)
    for n0 in range(0, ls * LANES, MXU_DIM):
        k_hi = n0 + MXU_DIM
        ycol = (jnp.dot(u_s[:, :k_hi], t_ref[:k_hi, n0:n0 + MXU_DIM], preferred_element_type=F32)
                + jnp.dot(hin, wout_ref[:, n0:n0 + MXU_DIM], preferred_element_type=F32))
        for t in range(n0 // LANES, (n0 + MXU_DIM) // LANES):
            piece = ycol[:, t * LANES - n0:(t + 1) * LANES - n0]
            sub_rows(y_ref, t, None)[...] = piece[0:rp]
            for j in range(ns):
                sub_rows(y_ref, t, j)[...] = piece[rp + j * nb:rp + (j + 1) * nb]


def ssm_mix(xn, n_prompt, n_batch, h0_re, h0_im, prep):
    rows, d = xn.shape
    g = d // SSM_GROUP
    ls = SSM_SUB
    tg = SSM_TILE_GROUPS
    nt = g // tg
    t_dec = (rows - n_prompt) // n_batch
    r = n_prompt // ls + (t_dec // ls) * n_batch
    kw = ls * LANES
    sw = 2 * tg * SSM_STATE
    c_t, c_win, c_out, a_rows = prep
    h0 = jnp.concatenate([h0_re.astype(F32).reshape(n_batch, nt, sw // 2),
                          h0_im.astype(F32).reshape(n_batch, nt, sw // 2)], axis=-1).transpose(1, 0, 2)

    mat = lambda a, b: pl.BlockSpec((None, a, b), lambda i: (i, 0, 0))
    col = pl.BlockSpec((rows, LANES), lambda i: (0, i))
    vmem = (4 * rows * LANES * 4 + (kw * kw + 2 * kw * sw) * 2 + r * kw * 2 + 2 * r * sw * 4
            + r * sw * 4 + 2 * kw * max(kw, sw) * 4)
    y, hp, hs = pl.pallas_call(
        functools.partial(_ssm_body, n_prompt, n_batch, t_dec),
        grid=(nt,),
        in_specs=[col, mat(kw, c_t.shape[2]), mat(kw, c_win.shape[2]), mat(sw, c_out.shape[2]),
                  mat(2, sw), mat(n_batch, sw)],
        out_specs=[col, mat(1, sw), mat(n_batch, sw)],
        out_shape=[jax.ShapeDtypeStruct((rows, d), F32), jax.ShapeDtypeStruct((nt, 1, sw), F32),
                   jax.ShapeDtypeStruct((nt, n_batch, sw), F32)],
        scratch_shapes=[pltpu.VMEM((r, kw), BF16), pltpu.VMEM((r, sw), F32), pltpu.VMEM((r, sw), F32),
                        pltpu.VMEM((kw, kw), BF16), pltpu.VMEM((kw, sw), BF16), pltpu.VMEM((sw, kw), BF16)],
        compiler_params=_params(("arbitrary",), vmem),
        name="ssm",
    )(xn, c_t, c_win, c_out, a_rows, h0)

    half = sw // 2
    re_p = hp[:, 0, :half].reshape(g, SSM_STATE)[None]
    im_p = hp[:, 0, half:].reshape(g, SSM_STATE)[None]
    re_s = hs[:, :, :half].transpose(1, 0, 2).reshape(n_batch, g, SSM_STATE)
    im_s = hs[:, :, half:].transpose(1, 0, 2).reshape(n_batch, g, SSM_STATE)
    return y, re_p, im_p, re_s, im_s


def _gelu_body(y_ref, xn_ref, d_ref, o_ref):
    y = y_ref[...] + d_ref[...] * xn_ref[...]
    c = math.sqrt(2.0 / math.pi)
    cdf = 0.5 * (1.0 + jnp.tanh(c * (y + 0.044715 * (y * y * y))))
    o_ref[...] = (y * cdf).astype(o_ref.dtype)


def skip_gelu(y, xn, d_skip):
    m, d = y.shape
    tm = min(NORM_TILE, m)
    row = pl.BlockSpec((tm, d), lambda i: (i, 0))
    return pl.pallas_call(
        _gelu_body,
        grid=(m // tm,),
        in_specs=[row, row, pl.BlockSpec((1, d), lambda i: (0, 0))],
        out_specs=row,
        out_shape=jax.ShapeDtypeStruct((m, d), BF16),
        compiler_params=_params(("arbitrary",), tm * d * (2 * 10 + 4 * 4)),
        name="skip_gelu",
    )(y, xn, d_skip.reshape(1, d).astype(F32))


_NT = (((1,), (1,)), ((), ()))


def _softmax_rows(s):
    m = jnp.max(s, axis=-1, keepdims=True)
    e = jnp.exp(s - m)
    return e / jnp.sum(e, axis=-1, keepdims=True)


def _memattn_body(q_ref, k_ref, v_ref, o_ref, kb, vb):
    @pl.when(pl.program_id(0) == 0)
    def _():
        kb[...] = k_ref[...].astype(BF16)
        vb[...] = v_ref[...].astype(BF16)

    dh = q_ref.shape[1] // MEM_HEADS
    scale = dh ** -0.5
    for h in range(MEM_HEADS):
        sl = slice(h * dh, (h + 1) * dh)
        s = lax.dot_general(q_ref[:, sl], kb[:, sl], _NT, preferred_element_type=F32) * scale
        p = _softmax_rows(s).astype(BF16)
        o_ref[:, sl] = jnp.dot(p, vb[:, sl], preferred_element_type=F32).astype(o_ref.dtype)


def _memattn_cache_body(q_ref, k_ref, v_ref, prev_ref, o_ref):
    del prev_ref
    n_heads, dh = k_ref.shape[1], k_ref.shape[2]
    scale = dh ** -0.5
    for h in range(n_heads):
        sl = slice(h * dh, (h + 1) * dh)
        s = lax.dot_general(q_ref[:, sl], k_ref[:, h, :].astype(BF16), _NT, preferred_element_type=F32) * scale
        p = _softmax_rows(s).astype(BF16)
        o_ref[:, sl] = jnp.dot(p, v_ref[:, h, :].astype(BF16), preferred_element_type=F32).astype(o_ref.dtype)


def mem_attention(q, kv_prompt, cache_k, cache_v, layer, n_prompt, n_batch):
    rows, d = q.shape
    mt = kv_prompt.shape[0]
    tm = min(ROW_TILE, n_prompt)
    t_dec = (rows - n_prompt) // n_batch
    scratch = [pltpu.VMEM((mt, d), BF16), pltpu.VMEM((mt, d), BF16)]
    vmem_kv = 2 * 2 * mt * d * 4 + 2 * mt * d * 2
    o = pl.pallas_call(
        _memattn_body,
        grid=(n_prompt // tm,),
        in_specs=[pl.BlockSpec((tm, d), lambda i: (i, 0)),
                  pl.BlockSpec((mt, d), lambda i: (0, 0)),
                  pl.BlockSpec((mt, d), lambda i: (0, 1))],
        out_specs=pl.BlockSpec((tm, d), lambda i: (i, 0)),
        out_shape=jax.ShapeDtypeStruct((rows, d), BF16),
        scratch_shapes=scratch,
        compiler_params=_params(("arbitrary",), vmem_kv + 4 * tm * d * 2 + 2 * tm * d * 4),
        name="mem_attn_prompt",
    )(q, kv_prompt, kv_prompt)
    pb = n_prompt // t_dec
    dh = d // MEM_HEADS
    slab = pl.BlockSpec((None, None, mt, MEM_HEADS, dh), lambda b: (layer, b, 0, 0, 0))
    return pl.pallas_call(
        _memattn_cache_body,
        grid=(n_batch,),
        in_specs=[pl.BlockSpec((t_dec, d), lambda b: (pb + b, 0)), slab, slab,
                  pl.BlockSpec(memory_space=pl.ANY)],
        out_specs=pl.BlockSpec((t_dec, d), lambda b: (pb + b, 0)),
        out_shape=jax.ShapeDtypeStruct((rows, d), BF16),
        input_output_aliases={3: 0},
        compiler_params=_params(("arbitrary",), 2 * 2 * 2 * mt * d * 4 + 8 * t_dec * d * 4),
        name="mem_attn_sample",
    )(q, cache_k, cache_v, o)


def _band_prompt_body(n_groups, q_ref, k_ref, v_ref, b_ref, o_ref, kb, vb):
    qb = pl.program_id(1)
    s_len, dh = k_ref.shape
    past = PAST_CHUNKS * CHUNK

    @pl.when(qb == 0)
    def _():
        for ref, src in ((kb, k_ref), (vb, v_ref)):
            ref[0:past, :] = jnp.zeros((past, dh), BF16)
            ref[past:past + s_len, :] = src[...].astype(BF16)

    bias = b_ref[...]
    col = lax.broadcasted_iota(jnp.int32, (BAND_ROWS, BAND_WIN), 1)
    scale = dh ** -0.5

    def group(gi, carry):
        c0 = (qb * n_groups + gi) * BAND_GROUP
        r0 = pl.multiple_of(gi * BAND_ROWS, BAND_ROWS)
        w0 = pl.multiple_of(c0 * CHUNK, BAND_ROWS)
        q = q_ref[pl.ds(r0, BAND_ROWS), :]
        s = lax.dot_general(q, kb[pl.ds(w0, BAND_WIN), :], _NT, preferred_element_type=F32) * scale + bias
        s = jnp.where(col >= (PAST_CHUNKS - c0) * CHUNK, s, NEG)
        p = _softmax_rows(s).astype(BF16)
        o_ref[pl.ds(r0, BAND_ROWS), :] = jnp.dot(p, vb[pl.ds(w0, BAND_WIN), :],
                                                 preferred_element_type=F32).astype(o_ref.dtype)
        return carry

    lax.fori_loop(0, n_groups, group, 0, unroll=True)


def _band_sample_body(hb, dh, q_ref, kc_ref, vc_ref, kn_ref, vn_ref, bc_ref, bn_ref, prev_ref, o_ref):
    del prev_ref
    scale = dh ** -0.5
    for h in range(hb):
        sl = slice(h * dh, (h + 1) * dh)
        q = q_ref[:, sl]
        s1 = lax.dot_general(q, kc_ref[:, h, :].astype(BF16), _NT, preferred_element_type=F32) * scale + bc_ref[h]
        s2 = lax.dot_general(q, kn_ref[:, sl].astype(BF16), _NT, preferred_element_type=F32) * scale + bn_ref[h]
        m = jnp.maximum(jnp.max(s1, axis=-1, keepdims=True), jnp.max(s2, axis=-1, keepdims=True))
        e1 = jnp.exp(s1 - m)
        e2 = jnp.exp(s2 - m)
        den = jnp.sum(e1, axis=-1, keepdims=True) + jnp.sum(e2, axis=-1, keepdims=True)
        o = (jnp.dot((e1 / den).astype(BF16), vc_ref[:, h, :].astype(BF16), preferred_element_type=F32)
             + jnp.dot((e2 / den).astype(BF16), vn_ref[:, sl].astype(BF16), preferred_element_type=F32))
        o_ref[:, sl] = o.astype(o_ref.dtype)


def _rel_bias(table, offset, n_q, n_k):
    p = n_q + n_k
    k = jnp.arange(p)
    diff = jnp.where(k < n_k, k, k - p)
    w = table[:, jnp.clip(offset - diff, -MAX_REL, MAX_REL) + MAX_REL].astype(F32)
    h = table.shape[0]
    flat = jnp.broadcast_to(w[:, None, :], (h, n_q, p)).reshape(h, n_q * p)
    return flat[:, :n_q * (p - 1)].reshape(h, n_q, p - 1)[:, :, :n_k]


def band_attention(q, kv, cache_k, cache_v, table, n_prompt, n_batch):
    rows, d = q.shape
    nh = N_HEADS
    dh = d // nh
    past = PAST_CHUNKS * CHUNK
    band = past + CHUNK
    t_dec = (rows - n_prompt) // n_batch
    cb = cache_k.shape[1]

    lo = (jnp.arange(BAND_ROWS)[:, None] // CHUNK) * CHUNK
    col = jnp.arange(BAND_WIN)[None, :]
    in_band = (col >= lo) & (col < lo + band)
    bias_p = jnp.where(in_band[None], _rel_bias(table, past, BAND_ROWS, BAND_WIN), NEG)
    bias_s = _rel_bias(table, cb, t_dec, cb + t_dec)
    bias_c, bias_n = bias_s[:, :, :cb], bias_s[:, :, cb:]

    bq = min(BAND_Q_ROWS, n_prompt)
    pad_rows = past + n_prompt
    o = pl.pallas_call(
        functools.partial(_band_prompt_body, bq // BAND_ROWS),
        grid=(nh, n_prompt // bq),
        in_specs=[pl.BlockSpec((bq, dh), lambda h, i: (i, h)),
                  pl.BlockSpec((n_prompt, dh), lambda h, i: (0, h)),
                  pl.BlockSpec((n_prompt, dh), lambda h, i: (0, nh + h)),
                  pl.BlockSpec((None, BAND_ROWS, BAND_WIN), lambda h, i: (h, 0, 0))],
        out_specs=pl.BlockSpec((bq, dh), lambda h, i: (i, h)),
        out_shape=jax.ShapeDtypeStruct((rows, d), BF16),
        scratch_shapes=[pltpu.VMEM((pad_rows, dh), BF16), pltpu.VMEM((pad_rows, dh), BF16)],
        compiler_params=_params(("arbitrary", "arbitrary"),
                                4 * n_prompt * dh * 4 + 2 * pad_rows * dh * 2 + 8 * bq * dh * 2),
        name="band_attn_prompt",
    )(q, kv, kv, bias_p)

    hb = min(SAMPLE_HEADS_PER_STEP, nh)
    ngrp = nh // hb
    pb = n_prompt // t_dec
    w = hb * dh
    return pl.pallas_call(
        functools.partial(_band_sample_body, hb, dh),
        grid=(n_batch, ngrp),
        in_specs=[pl.BlockSpec((t_dec, w), lambda b, g: (pb + b, g)),
                  pl.BlockSpec((None, cb, hb, dh), lambda b, g: (b, 0, g, 0)),
                  pl.BlockSpec((None, cb, hb, dh), lambda b, g: (b, 0, g, 0)),
                  pl.BlockSpec((t_dec, w), lambda b, g: (pb + b, g)),
                  pl.BlockSpec((t_dec, w), lambda b, g: (pb + b, ngrp + g)),
                  pl.BlockSpec((hb, t_dec, cb), lambda b, g: (g, 0, 0)),
                  pl.BlockSpec((hb, t_dec, t_dec), lambda b, g: (g, 0, 0)),
                  pl.BlockSpec(memory_space=pl.ANY)],
        out_specs=pl.BlockSpec((t_dec, w), lambda b, g: (pb + b, g)),
        out_shape=jax.ShapeDtypeStruct((rows, d), BF16),
        input_output_aliases={7: 0},
        compiler_params=_params(("arbitrary", "arbitrary"), 4 * cb * w * 4 + 16 * t_dec * w * 4 + 4 * hb * t_dec * cb * 4),
        name="band_attn_sample",
    )(q, cache_k, cache_v, kv, kv, bias_c, bias_n, o)


def _row_copy(src_hbm, dst_ref, sem, r, src_row):
    return pltpu.make_async_copy(src_hbm.at[pl.ds(src_row, 1), :], dst_ref.at[pl.ds(r, 1), :], sem)


def _issue_rows(src_hbm, idx_ref, dst_ref, n_rows, sem):
    def body(b, carry):
        for k in range(GATHER_UNROLL):
            r = b * GATHER_UNROLL + k
            _row_copy(src_hbm, dst_ref, sem, r, idx_ref[0, r]).start()
        return carry
    lax.fori_loop(0, n_rows // GATHER_UNROLL, body, 0)


def _drain_rows(src_hbm, dst_ref, n_rows, sem):
    def body(b, carry):
        for k in range(GATHER_UNROLL):
            _row_copy(src_hbm, dst_ref, sem, b * GATHER_UNROLL + k, 0).wait()
        return carry
    lax.fori_loop(0, n_rows // GATHER_UNROLL, body, 0)


def _moe_body(item_e, item_blk, item_nsub, slot_cur, slot_next, xn_hbm, wg_ref, wu_ref, wo_ref, o_ref,
              xbuf, sem):
    del item_e, item_blk
    i = pl.program_id(0)
    j = pl.program_id(1)
    n_items = pl.num_programs(0)
    nsub = item_nsub[i]
    slot = i % 2

    @pl.when((nsub > 0) & (j == 0))
    def _():
        @pl.when(i == 0)
        def _():
            _issue_rows(xn_hbm, slot_cur, xbuf.at[0], nsub * MOE_SUB, sem.at[0])

        _drain_rows(xn_hbm, xbuf.at[slot], nsub * MOE_SUB, sem.at[slot])
        nsub_next = item_nsub[jnp.minimum(i + 1, n_items - 1)]

        @pl.when((i + 1 < n_items) & (nsub_next > 0))
        def _():
            _issue_rows(xn_hbm, slot_next, xbuf.at[1 - slot], nsub_next * MOE_SUB, sem.at[1 - slot])

        o_ref[...] = jnp.zeros_like(o_ref)

    @pl.when(nsub > 0)
    def _():
        xcur = xbuf.at[slot]

        def sub(s, carry):
            r0 = pl.multiple_of(s * MOE_SUB, MOE_SUB)
            xb = xcur[pl.ds(r0, MOE_SUB), :]
            g = jnp.dot(xb, wg_ref[...], preferred_element_type=F32)
            u = jnp.dot(xb, wu_ref[...], preferred_element_type=F32)
            h = g * jax.nn.sigmoid(g) * u
            o_ref[pl.ds(r0, MOE_SUB), :] += jnp.dot(h, wo_ref[...], preferred_element_type=F32)
            return carry

        lax.fori_loop(0, nsub, sub, 0)


def moe_experts(xn, slot_tok, w_in, w_out, layer, item_e, item_blk, item_nsub):
    d = xn.shape[1]
    n_items = slot_tok.shape[0]
    n_slots = n_items * MOE_ROWS
    de = w_out.shape[2]
    tj = min(MOE_COL, de)
    nj = de // tj

    def jj(i, j, ie, ib, ns):
        return jnp.where(ns[i] > 0, j, nj - 1)

    grid_spec = pltpu.PrefetchScalarGridSpec(
        num_scalar_prefetch=3,
        grid=(n_items, nj),
        in_specs=[pl.BlockSpec((None, 1, MOE_ROWS), lambda i, j, ie, ib, ns: (ib[i], 0, 0),
                               memory_space=pltpu.SMEM),
                  pl.BlockSpec((None, 1, MOE_ROWS),
                               lambda i, j, ie, ib, ns: (ib[jnp.minimum(i + 1, n_items - 1)], 0, 0),
                               memory_space=pltpu.SMEM),
                  pl.BlockSpec(memory_space=pl.ANY),
                  pl.BlockSpec((None, None, d, tj), lambda i, j, ie, ib, ns: (layer, ie[i], 0, jj(i, j, ie, ib, ns))),
                  pl.BlockSpec((None, None, d, tj),
                               lambda i, j, ie, ib, ns: (layer, ie[i], 0, nj + jj(i, j, ie, ib, ns))),
                  pl.BlockSpec((None, None, tj, d), lambda i, j, ie, ib, ns: (layer, ie[i], jj(i, j, ie, ib, ns), 0))],
        out_specs=pl.BlockSpec((MOE_ROWS, d), lambda i, j, ie, ib, ns: (ib[i], 0)),
        scratch_shapes=[pltpu.VMEM((2, MOE_ROWS, d), F32), pltpu.SemaphoreType.DMA((2,))],
    )
    vmem = 4 * MOE_ROWS * d * 4 + 3 * d * tj * 2 * 4 + 4 * MOE_SUB * d * 4
    return pl.pallas_call(
        _moe_body,
        grid_spec=grid_spec,
        out_shape=jax.ShapeDtypeStruct((n_slots, d), F32),
        compiler_params=_params(("arbitrary", "arbitrary"), vmem),
        name="moe_experts",
    )(item_e, item_blk, item_nsub, slot_tok, slot_tok, xn, w_in, w_in, w_out)


def _route(logits):
    t = logits.shape[0]
    lg = logits[:, :N_GROUPS]
    g_idx = jnp.argmax(lg, axis=-1).astype(jnp.int32)[:, None]
    in_group = lax.broadcasted_iota(jnp.int32, (t, N_GROUPS), 1) == g_idx
    p_group = jnp.sum(jnp.where(in_group, jax.nn.softmax(lg, axis=-1), 0.0), axis=-1, keepdims=True)
    le = logits[:, N_GROUPS:N_GROUPS + N_EXPERTS].reshape(t, N_GROUPS, EXPERTS_PER_GROUP)
    le = jnp.sum(jnp.where(in_group[:, :, None], le, 0.0), axis=1)
    lane = lax.broadcasted_iota(jnp.int32, (t, EXPERTS_PER_GROUP), 1)
    tops_v, tops_j = [], []
    for _ in range(TOP_K):
        j = jnp.argmax(le, axis=-1).astype(jnp.int32)[:, None]
        tops_j.append(j)
        tops_v.append(jnp.max(le, axis=-1, keepdims=True))
        le = jnp.where(lane == j, -jnp.inf, le)
    top_v = jnp.concatenate(tops_v, axis=-1)
    top_j = jnp.concatenate(tops_j, axis=-1)
    gate = p_group * jax.nn.softmax(top_v, axis=-1)
    eidx = g_idx * EXPERTS_PER_GROUP + top_j
    return eidx.astype(jnp.int32), gate


def _dispatch(eidx):
    t = eidx.shape[0]
    a = t * TOP_K
    n_items = a // MOE_ROWS + N_EXPERTS
    flat_e = eidx.reshape(a)
    flat_tok = jnp.repeat(jnp.arange(t, dtype=jnp.int32), TOP_K)
    order = jnp.argsort(flat_e)
    se = flat_e[order]
    counts = jnp.zeros((N_EXPERTS,), jnp.int32).at[flat_e].add(1)
    nblk = (counts + MOE_ROWS - 1) // MOE_ROWS
    blk_end = jnp.cumsum(nblk)
    blk_start = blk_end - nblk
    start = jnp.cumsum(counts) - counts
    rank = jnp.arange(a, dtype=jnp.int32) - start[se]
    dest_sorted = blk_start[se] * MOE_ROWS + rank
    slot_tok = jnp.zeros((n_items * MOE_ROWS,), jnp.int32).at[dest_sorted].set(flat_tok[order])
    dest = jnp.zeros((a,), jnp.int32).at[order].set(dest_sorted).reshape(t, TOP_K)
    n_used = blk_end[-1]
    item = jnp.arange(n_items, dtype=jnp.int32)
    item_c = jnp.minimum(item, n_used - 1)
    item_e = jnp.minimum(jnp.searchsorted(blk_end, item_c, side='right'), N_EXPERTS - 1).astype(jnp.int32)
    rows_left = counts[item_e] - (item_c - blk_start[item_e]) * MOE_ROWS
    nsub = (jnp.clip(rows_left, 0, MOE_ROWS) + MOE_SUB - 1) // MOE_SUB
    item_nsub = jnp.where(item < n_used, nsub, 0).astype(jnp.int32)
    return slot_tok.reshape(n_items, 1, MOE_ROWS), dest, item_e, item_c.astype(jnp.int32), item_nsub


def _combine_body(n_blocks, d_cur, d_next, gate_ref, x_ref, ys_hbm, o_ref, buf, sem):
    i = pl.program_id(0)
    slot = i % 2
    tb = x_ref.shape[0]

    rows_per_iter = GATHER_UNROLL // TOP_K

    def row_copy(s, k, r, src_row):
        return _row_copy(ys_hbm, buf.at[s, k], sem.at[s], r, src_row)

    def issue(idx_ref, s):
        def body(b, carry):
            for u in range(rows_per_iter):
                r = b * rows_per_iter + u
                for k in range(TOP_K):
                    row_copy(s, k, r, idx_ref[k, r]).start()
            return carry
        lax.fori_loop(0, tb // rows_per_iter, body, 0)

    @pl.when(i == 0)
    def _():
        issue(d_cur, 0)

    @pl.when(i + 1 < n_blocks)
    def _():
        issue(d_next, 1 - slot)

    def drain(b, carry):
        for u in range(rows_per_iter):
            for k in range(TOP_K):
                row_copy(slot, k, b * rows_per_iter + u, 0).wait()
        return carry

    lax.fori_loop(0, tb // rows_per_iter, drain, 0)
    acc = buf[slot, 0] * gate_ref[:, 0:1]
    for k in range(1, TOP_K):
        acc = acc + buf[slot, k] * gate_ref[:, k:k + 1]
    o_ref[...] = x_ref[...] + acc


def moe_combine(x, ys, dest, gate):
    t, d = x.shape
    tb = min(COMBINE_ROWS, t)
    n_blocks = t // tb
    dest3 = dest.reshape(n_blocks, tb, TOP_K).transpose(0, 2, 1)
    idx = lambda f: pl.BlockSpec((None, TOP_K, tb), lambda i: (f(i), 0, 0), memory_space=pltpu.SMEM)
    row = pl.BlockSpec((tb, d), lambda i: (i, 0))
    return pl.pallas_call(
        functools.partial(_combine_body, n_blocks),
        grid=(n_blocks,),
        in_specs=[idx(lambda i: i), idx(lambda i: jnp.minimum(i + 1, n_blocks - 1)),
                  pl.BlockSpec((tb, TOP_K), lambda i: (i, 0)), row, pl.BlockSpec(memory_space=pl.ANY)],
        out_specs=row,
        out_shape=jax.ShapeDtypeStruct((t, d), F32),
        scratch_shapes=[pltpu.VMEM((2, TOP_K, tb, d), F32), pltpu.SemaphoreType.DMA((2,))],
        compiler_params=_params(("arbitrary",), (2 * TOP_K + 6) * tb * d * 4),
        name="moe_combine",
    )(dest3, dest3, gate, x, ys)


def moe_ffn(x, xn, logits, w_in, w_out, layer):
    eidx, gate = _route(logits)
    slot_tok, dest, item_e, item_blk, item_nsub = _dispatch(eidx)
    ys = moe_experts(xn, slot_tok, w_in, w_out, layer, item_e, item_blk, item_nsub)
    return moe_combine(x, ys, dest, gate)


def _router_weights(w_group, b_group, w_expert, b_expert):
    d = w_group.shape[0]
    pad = LANES - N_GROUPS - N_EXPERTS
    w = jnp.concatenate([w_group, w_expert, jnp.zeros((d, pad), w_group.dtype)], axis=1).astype(BF16)
    b = jnp.concatenate([b_group.astype(F32), b_expert.astype(F32), jnp.zeros((pad,), F32)]).reshape(1, LANES)
    return w, b


def kernel(x_prompt, x_sample, state_ssm_re, state_ssm_im, cache_band_k, cache_band_v, cache_mem_k, cache_mem_v, mem_prompt, norm_mix, norm_mem, norm_memin, norm_ffn, norm_kv, norm_final, ssm_a_re, ssm_a_im, ssm_log_dt, ssm_b_re, ssm_b_im, ssm_c_re, ssm_c_im, ssm_d, ssm_w_glu, w_kv_shared, attn_w_q, attn_rel_bias, attn_w_o, mem_w_q, mem_w_kv, mem_w_o, moe_w_group, moe_b_group, moe_w_expert, moe_b_expert, moe_w_in, moe_w_out):
    bp, seq, d = x_prompt.shape
    nb, t_dec, _ = x_sample.shape
    assert bp == 1
    n_prompt = bp * seq
    depth = norm_mix.shape[0]
    n_a = ssm_a_re.shape[0]
    dh = d // N_HEADS
    mt = mem_prompt.shape[1]
    x = jnp.concatenate([x_prompt.reshape(n_prompt, d), x_sample.reshape(nb * t_dec, d)], axis=0)
    mem = mem_prompt.reshape(mt, d)

    out_re_p, out_im_p, out_re_s, out_im_s, out_mk, out_mv = [], [], [], [], [], []
    kv = None
    for layer in range(depth):
        if layer < n_a:
            a = layer
            (xn,) = rmsnorm(x, [norm_mix[layer]], [F32])
            prep = _ssm_prep(ssm_a_re[a], ssm_a_im[a], ssm_log_dt[a], ssm_b_re[a], ssm_b_im[a],
                             ssm_c_re[a], ssm_c_im[a])
            y, re_p, im_p, re_s, im_s = ssm_mix(xn, n_prompt, nb, state_ssm_re[a], state_ssm_im[a], prep)
            out_re_p.append(re_p)
            out_im_p.append(im_p)
            out_re_s.append(re_s)
            out_im_s.append(im_s)
            g = skip_gelu(y, xn, ssm_d[a])
            x = matmul(g, ssm_w_glu, prefix=(a,), n_out=d, glu=True, res=x, name="glu_proj")
        else:
            b = layer - n_a
            if layer == n_a:
                kvn, xn = rmsnorm(x, [norm_kv, norm_mix[layer]], [BF16, BF16])
                kv = matmul(kvn, w_kv_shared, name="kv_proj")
            else:
                (xn,) = rmsnorm(x, [norm_mix[layer]], [BF16])
            q = matmul(xn, attn_w_q, prefix=(b,), out_dtype=BF16, name="attn_q_proj")
            o = band_attention(q, kv, cache_band_k, cache_band_v, attn_rel_bias[b], n_prompt, nb)
            x = matmul(o, attn_w_o, prefix=(b,), res=x, name="attn_o_proj")

        (memn,) = rmsnorm(mem, [norm_memin[layer]], [BF16])
        kv_mem = matmul(memn, mem_w_kv, prefix=(layer,), name="mem_kv_proj")
        out_mk.append(kv_mem[:, :d].reshape(bp, mt, MEM_HEADS, d // MEM_HEADS))
        out_mv.append(kv_mem[:, d:].reshape(bp, mt, MEM_HEADS, d // MEM_HEADS))
        (xn,) = rmsnorm(x, [norm_mem[layer]], [BF16])
        q = matmul(xn, mem_w_q, prefix=(layer,), out_dtype=BF16, name="mem_q_proj")
        o = mem_attention(q, kv_mem, cache_mem_k, cache_mem_v, layer, n_prompt, nb)
        x = matmul(o, mem_w_o, prefix=(layer,), res=x, name="mem_o_proj")

        w_r, b_r = _router_weights(moe_w_group[layer], moe_b_group[layer], moe_w_expert[layer], moe_b_expert[layer])
        xn, logits = rmsnorm_router(x, norm_ffn[layer], w_r, b_r)
        x = moe_ffn(x, xn, logits, moe_w_in, moe_w_out, layer)

    y_p, y_s = rmsnorm_final(x, norm_final, n_prompt)
    keep = min(PAST_CHUNKS * CHUNK, seq)
    k_all, v_all = kv[:, :d], kv[:, d:]
    bk_p = k_all[n_prompt - keep:n_prompt].reshape(bp, keep, N_HEADS, dh)
    bv_p = v_all[n_prompt - keep:n_prompt].reshape(bp, keep, N_HEADS, dh)
    keep_s = min(PAST_CHUNKS * CHUNK, t_dec)
    bk_s = k_all[n_prompt:].reshape(nb, t_dec, N_HEADS, dh)[:, t_dec - keep_s:]
    bv_s = v_all[n_prompt:].reshape(nb, t_dec, N_HEADS, dh)[:, t_dec - keep_s:]
    return (y_p.reshape(bp, seq, d), y_s.reshape(nb, t_dec, d),
            jnp.stack(out_re_p), jnp.stack(out_im_p), jnp.stack(out_re_s), jnp.stack(out_im_s),
            bk_p, bv_p, bk_s, bv_s, jnp.stack(out_mk), jnp.stack(out_mv))
```

```python
import functools
import math

import jax
import jax.numpy as jnp
from jax import lax
from jax.experimental import pallas as pl
from jax.experimental.pallas import tpu as pltpu

F32 = jnp.float32
BF16 = jnp.bfloat16

CHUNK = 64
RMS_EPS = 1e-6
SSM_GROUP = 16
SSM_STATE = 64
N_HEADS = 32
PAST_CHUNKS = 8
MAX_REL = 256
MEM_HEADS = 4
N_GROUPS = 8
EXPERTS_PER_GROUP = 8
N_EXPERTS = N_GROUPS * EXPERTS_PER_GROUP
TOP_K = 2

V7X_VMEM_BYTES = 64 * 1024 * 1024
LANES = 128
MXU_DIM = 256

BF16_ROWS = 16
ROW_TILE = 512
MATMUL_ROWS_MAX = 1152
NORM_TILE = 256
COL_TILE = 512
SSM_SUB = 8
SSM_TILE_GROUPS = LANES // SSM_GROUP
BAND_Q_ROWS = 1024
BAND_GROUP = 4
BAND_ROWS = BAND_GROUP * CHUNK
BAND_WIN = (PAST_CHUNKS + BAND_GROUP) * CHUNK
SAMPLE_HEADS_PER_STEP = 8
MOE_ROWS = 384
MOE_SUB = 128
MOE_COL = 256
COMBINE_ROWS = 128
GATHER_UNROLL = 8
GATHER_DMA_PRIORITY = 1
NEG = -1e30


def _params(sem, vmem_bytes):
    limit = min(int(vmem_bytes) + (4 << 20), V7X_VMEM_BYTES - (8 << 20))
    return pltpu.CompilerParams(dimension_semantics=sem, vmem_limit_bytes=limit)


def _rms_body(n_out, x_ref, *refs):
    g_refs, o_refs = refs[:n_out], refs[n_out:]
    x = x_ref[...]
    y = x * lax.rsqrt(jnp.mean(x * x, axis=-1, keepdims=True) + RMS_EPS)
    for g_ref, o_ref in zip(g_refs, o_refs):
        o_ref[...] = (y * g_ref[...]).astype(o_ref.dtype)


def rmsnorm(x, gains, dtypes):
    m, d = x.shape
    tm = min(NORM_TILE, m)
    n = len(gains)
    row = pl.BlockSpec((tm, d), lambda i: (i, 0))
    vec = pl.BlockSpec((1, d), lambda i: (0, 0))
    outs = pl.pallas_call(
        functools.partial(_rms_body, n),
        grid=(m // tm,),
        in_specs=[row] + [vec] * n,
        out_specs=[row] * n,
        out_shape=[jax.ShapeDtypeStruct((m, d), dt) for dt in dtypes],
        compiler_params=_params(("arbitrary",), tm * d * 4 * (2 * (1 + n) + 3)),
        name="rmsnorm",
    )(x, *[g.reshape(1, d).astype(F32) for g in gains])
    return outs


def _rms_router_body(x_ref, g_ref, wr_ref, br_ref, xn_ref, lg_ref):
    x = x_ref[...]
    y = x * lax.rsqrt(jnp.mean(x * x, axis=-1, keepdims=True) + RMS_EPS)
    xn = y * g_ref[...]
    xn_ref[...] = xn
    lg_ref[...] = jnp.dot(xn.astype(BF16), wr_ref[...], preferred_element_type=F32) + br_ref[...]


def rmsnorm_router(x, gain, w_router, b_router):
    m, d = x.shape
    tm = min(NORM_TILE, m)
    nr = w_router.shape[1]
    return pl.pallas_call(
        _rms_router_body,
        grid=(m // tm,),
        in_specs=[pl.BlockSpec((tm, d), lambda i: (i, 0)), pl.BlockSpec((1, d), lambda i: (0, 0)),
                  pl.BlockSpec((d, nr), lambda i: (0, 0)), pl.BlockSpec((1, nr), lambda i: (0, 0))],
        out_specs=[pl.BlockSpec((tm, d), lambda i: (i, 0)), pl.BlockSpec((tm, nr), lambda i: (i, 0))],
        out_shape=[jax.ShapeDtypeStruct((m, d), F32), jax.ShapeDtypeStruct((m, nr), F32)],
        compiler_params=_params(("arbitrary",), tm * d * (2 * 8 + 3 * 4) + 4 * d * nr),
        name="rmsnorm_router",
    )(x, gain.reshape(1, d).astype(F32), w_router, b_router)


def _mm_body(glu, has_res, x_ref, *refs):
    nw = 2 if glu else 1
    w_refs = refs[:nw]
    res_ref = refs[nw] if has_res else None
    o_ref = refs[nw + (1 if has_res else 0)]
    wb_refs = refs[nw + (1 if has_res else 0) + 1:]

    @pl.when(pl.program_id(1) == 0)
    def _():
        for w_ref, wb_ref in zip(w_refs, wb_refs):
            wb_ref[...] = w_ref[...].astype(BF16)

    x = x_ref[...]
    acc = jnp.dot(x, wb_refs[0][...], preferred_element_type=F32)
    if glu:
        gate = jnp.dot(x, wb_refs[1][...], preferred_element_type=F32)
        acc = acc * jax.nn.sigmoid(gate)
    if has_res:
        acc = res_ref[...] + acc
    o_ref[...] = acc.astype(o_ref.dtype)


def _matmul_rows(m):
    best = None
    for t in range(BF16_ROWS, min(m, MATMUL_ROWS_MAX) + 1, BF16_ROWS):
        if m % t == 0:
            best = t
    return best if best is not None else m


def matmul(x, w, prefix=(), n_out=None, glu=False, res=None, out_dtype=F32, name="matmul"):
    m, k = x.shape
    n = n_out if n_out is not None else w.shape[-1]
    tm = _matmul_rows(m)
    tn = min(COL_TILE // 2 if glu else COL_TILE, n)
    lead = (None,) * len(prefix)
    nblk = n // tn
    w_specs = [pl.BlockSpec(lead + (k, tn), lambda j, i: prefix + (0, j))]
    if glu:
        w_specs.append(pl.BlockSpec(lead + (k, tn), lambda j, i: prefix + (0, nblk + j)))
    nw = len(w_specs)
    in_specs = [pl.BlockSpec((tm, k), lambda j, i: (i, 0))] + w_specs
    args = [x] + [w] * nw
    if res is not None:
        in_specs.append(pl.BlockSpec((tm, tn), lambda j, i: (i, j)))
        args.append(res)
    osize = jnp.dtype(out_dtype).itemsize
    vmem = 2 * tm * k * 2 + nw * (2 * k * tn * 4 + k * tn * 2) + 2 * tm * tn * (osize + 4) + 4 * tm * tn * 4
    return pl.pallas_call(
        functools.partial(_mm_body, glu, res is not None),
        grid=(nblk, m // tm),
        in_specs=in_specs,
        out_specs=pl.BlockSpec((tm, tn), lambda j, i: (i, j)),
        out_shape=jax.ShapeDtypeStruct((m, n), out_dtype),
        scratch_shapes=[pltpu.VMEM((k, tn), BF16)] * nw,
        compiler_params=_params(("arbitrary", "arbitrary"), vmem),
        name=name,
    )(*args)


def _ssm_prep(a_re, a_im, log_dt, b_re, b_im, c_re, c_im):
    hi = lax.Precision.HIGHEST
    ls = SSM_SUB
    a_re = a_re.astype(F32)
    a_im = a_im.astype(F32)
    dt = jnp.exp(log_dt.astype(F32))[:, None]
    mag = jnp.exp(a_re * dt)
    ab_re = mag * jnp.cos(a_im * dt)
    ab_im = mag * jnp.sin(a_im * dt)
    den = a_re * a_re + a_im * a_im
    zr = ab_re - 1.0
    f_re = (zr * a_re + ab_im * a_im) / den
    f_im = (ab_im * a_re - zr * a_im) / den
    b_re = b_re.astype(F32)
    b_im = b_im.astype(F32)
    bb_re = f_re[..., None] * b_re - f_im[..., None] * b_im
    bb_im = f_re[..., None] * b_im + f_im[..., None] * b_re
    c_re = c_re.astype(F32)
    c_im = c_im.astype(F32)

    pw_re = [jnp.ones_like(ab_re)]
    pw_im = [jnp.zeros_like(ab_im)]
    for _ in range(ls):
        pr, pi = pw_re[-1], pw_im[-1]
        pw_re.append(pr * ab_re - pi * ab_im)
        pw_im.append(pr * ab_im + pi * ab_re)
    p_re = jnp.stack(pw_re, axis=1)
    p_im = jnp.stack(pw_im, axis=1)
    g = a_re.shape[0]

    cp_re = c_re[:, None] * p_re[:, :, None, :] - c_im[:, None] * p_im[:, :, None, :]
    cp_im = c_re[:, None] * p_im[:, :, None, :] + c_im[:, None] * p_re[:, :, None, :]
    kern = (jnp.einsum('gtpn,gnq->gtpq', cp_re[:, :ls], bb_re, precision=hi)
            - jnp.einsum('gtpn,gnq->gtpq', cp_im[:, :ls], bb_im, precision=hi))
    s_idx = jnp.arange(ls)[:, None]
    t_idx = jnp.arange(ls)[None, :]
    lag = t_idx - s_idx
    toep = kern[:, jnp.clip(lag, 0, ls - 1)]
    toep = jnp.where((lag >= 0)[None, :, :, None, None], toep, 0.0)
    toep = toep.transpose(0, 1, 4, 2, 3).reshape(g, ls * SSM_GROUP, ls * SSM_GROUP)

    pr_rev = p_re[:, ls - 1::-1][:, :, None, :]
    pi_rev = p_im[:, ls - 1::-1][:, :, None, :]
    bbt_re = bb_re.transpose(0, 2, 1)[:, None]
    bbt_im = bb_im.transpose(0, 2, 1)[:, None]
    win_re = pr_rev * bbt_re - pi_rev * bbt_im
    win_im = pr_rev * bbt_im + pi_rev * bbt_re
    win = jnp.concatenate([win_re, win_im], axis=-1).reshape(g, ls * SSM_GROUP, 2 * SSM_STATE)

    wo_re = cp_re[:, 1:].transpose(0, 3, 1, 2).reshape(g, SSM_STATE, ls * SSM_GROUP)
    wo_im = -cp_im[:, 1:].transpose(0, 3, 1, 2).reshape(g, SSM_STATE, ls * SSM_GROUP)
    wout = jnp.concatenate([wo_re, wo_im], axis=1)

    tg = SSM_TILE_GROUPS
    nt = g // tg
    lp = ls * SSM_GROUP
    c_t = toep.astype(BF16).reshape(nt, tg, ls, SSM_GROUP, lp).transpose(0, 2, 1, 3, 4).reshape(nt, ls * LANES, lp)
    c_win = win.astype(BF16).reshape(nt, tg, ls, SSM_GROUP, 2 * SSM_STATE).transpose(0, 2, 1, 3, 4).reshape(
        nt, ls * LANES, 2 * SSM_STATE)
    c_out = wout.astype(BF16).reshape(nt, tg, 2, SSM_STATE, lp).transpose(0, 2, 1, 3, 4).reshape(
        nt, 2 * tg * SSM_STATE, lp)
    al_re = p_re[:, ls].reshape(nt, tg * SSM_STATE)
    al_im = p_im[:, ls].reshape(nt, tg * SSM_STATE)
    a_rows = jnp.stack([jnp.concatenate([al_re, al_re], -1), jnp.concatenate([-al_im, al_im], -1)], axis=1)
    return c_t, c_win, c_out, a_rows


def _spread_groups(c, row_unit, col_unit):
    tg = SSM_TILE_GROUPS
    r, w = c.shape
    wide = w * tg
    src = lax.broadcasted_iota(jnp.int32, (w, wide), 0)
    dst = lax.broadcasted_iota(jnp.int32, (w, wide), 1)
    sel = ((src // col_unit == dst // (tg * col_unit)) & (src % col_unit == dst % col_unit)).astype(BF16)
    rep = jnp.dot(c, sel, preferred_element_type=F32)
    row_g = (lax.broadcasted_iota(jnp.int32, (r, wide), 0) // row_unit) % tg
    col_g = (lax.broadcasted_iota(jnp.int32, (r, wide), 1) // col_unit) % tg
    return jnp.where(row_g == col_g, rep, 0.0).astype(BF16)


def _ssm_body(n_prompt, nb, t_dec, xn_ref, ct_ref, cwin_ref, cout_ref, a_ref, h0_ref,
              y_ref, hp_ref, hs_ref, u_s, v_s, hin_s, t_ref, win_ref, wout_ref):
    ls = SSM_SUB
    rp = n_prompt // ls
    ns = t_dec // ls
    half = v_s.shape[1] // 2
    t_ref[...] = _spread_groups(ct_ref[...], SSM_GROUP, SSM_GROUP)
    win_ref[...] = _spread_groups(cwin_ref[...], SSM_GROUP, SSM_STATE)
    wout_ref[...] = _spread_groups(cout_ref[...], SSM_STATE, SSM_GROUP)

    def sub_rows(ref, s, j):
        if j is None:
            return ref.at[pl.ds(s, rp, stride=ls), :]
        return ref.at[pl.ds(n_prompt + j * ls + s, nb, stride=t_dec), :]

    for s in range(ls):
        cols = slice(s * LANES, (s + 1) * LANES)
        u_s[0:rp, cols] = sub_rows(xn_ref, s, None)[...].astype(BF16)
        for j in range(ns):
            u_s[rp + j * nb:rp + (j + 1) * nb, cols] = sub_rows(xn_ref, s, j)[...].astype(BF16)

    v_s[...] = jnp.dot(u_s[...], win_ref[...], preferred_element_type=F32)
    a1 = a_ref[0:1, :]
    a2 = a_ref[1:2, :]

    def step(h, v):
        h_sw = jnp.concatenate([h[:, half:], h[:, :half]], axis=1)
        return h * a1 + h_sw * a2 + v

    def body(c, h):
        hin_s[pl.ds(c, 1), :] = h
        return step(h, v_s[pl.ds(c, 1), :])

    h = lax.fori_loop(0, rp, body, jnp.zeros((1, 2 * half), F32), unroll=4)
    hp_ref[...] = h
    h = h0_ref[...]
    for j in range(ns):
        r0 = rp + j * nb
        hin_s[r0:r0 + nb, :] = h
        h = step(h, v_s[r0:r0 + nb, :])
    hs_ref[...] = h

    hin = hin_s[...].astype(BF16)
    for n0 in range(0, ls * LANES, MXU_DIM):
        k_hi = n0 + MXU_DIM
        ycol = (jnp.dot(u_s[:, :k_hi], t_ref[:k_hi, n0:n0 + MXU_DIM], preferred_element_type=F32)
                + jnp.dot(hin, wout_ref[:, n0:n0 + MXU_DIM], preferred_element_type=F32))
        for t in range(n0 // LANES, (n0 + MXU_DIM) // LANES):
            piece = ycol[:, t * LANES - n0:(t + 1) * LANES - n0]
            sub_rows(y_ref, t, None)[...] = piece[0:rp]
            for j in range(ns):
                sub_rows(y_ref, t, j)[...] = piece[rp + j * nb:rp + (j + 1) * nb]


def ssm_mix(xn, n_prompt, n_batch, h0_re, h0_im, prep):
    rows, d = xn.shape
    g = d // SSM_GROUP
    ls = SSM_SUB
    tg = SSM_TILE_GROUPS
    nt = g // tg
    t_dec = (rows - n_prompt) // n_batch
    r = n_prompt // ls + (t_dec // ls) * n_batch
    kw = ls * LANES
    sw = 2 * tg * SSM_STATE
    c_t, c_win, c_out, a_rows = prep
    h0 = jnp.concatenate([h0_re.astype(F32).reshape(n_batch, nt, sw // 2),
                          h0_im.astype(F32).reshape(n_batch, nt, sw // 2)], axis=-1).transpose(1, 0, 2)

    mat = lambda a, b: pl.BlockSpec((None, a, b), lambda i: (i, 0, 0))
    col = pl.BlockSpec((rows, LANES), lambda i: (0, i))
    vmem = (4 * rows * LANES * 4 + (kw * kw + 2 * kw * sw) * 2 + r * kw * 2 + 2 * r * sw * 4
            + r * sw * 4 + 2 * kw * max(kw, sw) * 4)
    y, hp, hs = pl.pallas_call(
        functools.partial(_ssm_body, n_prompt, n_batch, t_dec),
        grid=(nt,),
        in_specs=[col, mat(kw, c_t.shape[2]), mat(kw, c_win.shape[2]), mat(sw, c_out.shape[2]),
                  mat(2, sw), mat(n_batch, sw)],
        out_specs=[col, mat(1, sw), mat(n_batch, sw)],
        out_shape=[jax.ShapeDtypeStruct((rows, d), F32), jax.ShapeDtypeStruct((nt, 1, sw), F32),
                   jax.ShapeDtypeStruct((nt, n_batch, sw), F32)],
        scratch_shapes=[pltpu.VMEM((r, kw), BF16), pltpu.VMEM((r, sw), F32), pltpu.VMEM((r, sw), F32),
                        pltpu.VMEM((kw, kw), BF16), pltpu.VMEM((kw, sw), BF16), pltpu.VMEM((sw, kw), BF16)],
        compiler_params=_params(("arbitrary",), vmem),
        name="ssm",
    )(xn, c_t, c_win, c_out, a_rows, h0)

    half = sw // 2
    re_p = hp[:, 0, :half].reshape(g, SSM_STATE)[None]
    im_p = hp[:, 0, half:].reshape(g, SSM_STATE)[None]
    re_s = hs[:, :, :half].transpose(1, 0, 2).reshape(n_batch, g, SSM_STATE)
    im_s = hs[:, :, half:].transpose(1, 0, 2).reshape(n_batch, g, SSM_STATE)
    return y, re_p, im_p, re_s, im_s


def _gelu_body(y_ref, xn_ref, d_ref, o_ref):
    y = y_ref[...] + d_ref[...] * xn_ref[...]
    c = math.sqrt(2.0 / math.pi)
    cdf = 0.5 * (1.0 + jnp.tanh(c * (y + 0.044715 * (y * y * y))))
    o_ref[...] = (y * cdf).astype(o_ref.dtype)


def skip_gelu(y, xn, d_skip):
    m, d = y.shape
    tm = min(NORM_TILE, m)
    row = pl.BlockSpec((tm, d), lambda i: (i, 0))
    return pl.pallas_call(
        _gelu_body,
        grid=(m // tm,),
        in_specs=[row, row, pl.BlockSpec((1, d), lambda i: (0, 0))],
        out_specs=row,
        out_shape=jax.ShapeDtypeStruct((m, d), BF16),
        compiler_params=_params(("arbitrary",), tm * d * (2 * 10 + 4 * 4)),
        name="skip_gelu",
    )(y, xn, d_skip.reshape(1, d).astype(F32))


_NT = (((1,), (1,)), ((), ()))


def _softmax_rows(s):
    m = jnp.max(s, axis=-1, keepdims=True)
    e = jnp.exp(s - m)
    return e / jnp.sum(e, axis=-1, keepdims=True)


def _memattn_body(q_ref, k_ref, v_ref, o_ref, kb, vb):
    @pl.when(pl.program_id(0) == 0)
    def _():
        kb[...] = k_ref[...].astype(BF16)
        vb[...] = v_ref[...].astype(BF16)

    dh = q_ref.shape[1] // MEM_HEADS
    scale = dh ** -0.5
    for h in range(MEM_HEADS):
        sl = slice(h * dh, (h + 1) * dh)
        s = lax.dot_general(q_ref[:, sl], kb[:, sl], _NT, preferred_element_type=F32) * scale
        p = _softmax_rows(s).astype(BF16)
        o_ref[:, sl] = jnp.dot(p, vb[:, sl], preferred_element_type=F32).astype(o_ref.dtype)


def _memattn_cache_body(q_ref, k_ref, v_ref, prev_ref, o_ref):
    del prev_ref
    n_heads, dh = k_ref.shape[1], k_ref.shape[2]
    scale = dh ** -0.5
    for h in range(n_heads):
        sl = slice(h * dh, (h + 1) * dh)
        s = lax.dot_general(q_ref[:, sl], k_ref[:, h, :].astype(BF16), _NT, preferred_element_type=F32) * scale
        p = _softmax_rows(s).astype(BF16)
        o_ref[:, sl] = jnp.dot(p, v_ref[:, h, :].astype(BF16), preferred_element_type=F32).astype(o_ref.dtype)


def mem_attention(q, kv_prompt, cache_k, cache_v, layer, n_prompt, n_batch):
    rows, d = q.shape
    mt = kv_prompt.shape[0]
    tm = min(ROW_TILE, n_prompt)
    t_dec = (rows - n_prompt) // n_batch
    scratch = [pltpu.VMEM((mt, d), BF16), pltpu.VMEM((mt, d), BF16)]
    vmem_kv = 2 * 2 * mt * d * 4 + 2 * mt * d * 2
    o = pl.pallas_call(
        _memattn_body,
        grid=(n_prompt // tm,),
        in_specs=[pl.BlockSpec((tm, d), lambda i: (i, 0)),
                  pl.BlockSpec((mt, d), lambda i: (0, 0)),
                  pl.BlockSpec((mt, d), lambda i: (0, 1))],
        out_specs=pl.BlockSpec((tm, d), lambda i: (i, 0)),
        out_shape=jax.ShapeDtypeStruct((rows, d), BF16),
        scratch_shapes=scratch,
        compiler_params=_params(("arbitrary",), vmem_kv + 4 * tm * d * 2 + 2 * tm * d * 4),
        name="mem_attn_prompt",
    )(q, kv_prompt, kv_prompt)
    pb = n_prompt // t_dec
    dh = d // MEM_HEADS
    slab = pl.BlockSpec((None, None, mt, MEM_HEADS, dh), lambda b: (layer, b, 0, 0, 0))
    return pl.pallas_call(
        _memattn_cache_body,
        grid=(n_batch,),
        in_specs=[pl.BlockSpec((t_dec, d), lambda b: (pb + b, 0)), slab, slab,
                  pl.BlockSpec(memory_space=pl.ANY)],
        out_specs=pl.BlockSpec((t_dec, d), lambda b: (pb + b, 0)),
        out_shape=jax.ShapeDtypeStruct((rows, d), BF16),
        input_output_aliases={3: 0},
        compiler_params=_params(("arbitrary",), 2 * 2 * 2 * mt * d * 4 + 8 * t_dec * d * 4),
        name="mem_attn_sample",
    )(q, cache_k, cache_v, o)


def _band_prompt_body(n_groups, q_ref, k_ref, v_ref, b_ref, o_ref, kb, vb):
    qb = pl.program_id(1)
    s_len, dh = k_ref.shape
    past = PAST_CHUNKS * CHUNK

    @pl.when(qb == 0)
    def _():
        for ref, src in ((kb, k_ref), (vb, v_ref)):
            ref[0:past, :] = jnp.zeros((past, dh), BF16)
            ref[past:past + s_len, :] = src[...].astype(BF16)

    bias = b_ref[...]
    col = lax.broadcasted_iota(jnp.int32, (BAND_ROWS, BAND_WIN), 1)
    scale = dh ** -0.5

    def group(gi, carry):
        c0 = (qb * n_groups + gi) * BAND_GROUP
        r0 = pl.multiple_of(gi * BAND_ROWS, BAND_ROWS)
        w0 = pl.multiple_of(c0 * CHUNK, BAND_ROWS)
        q = q_ref[pl.ds(r0, BAND_ROWS), :]
        s = lax.dot_general(q, kb[pl.ds(w0, BAND_WIN), :], _NT, preferred_element_type=F32) * scale + bias
        s = jnp.where(col >= (PAST_CHUNKS - c0) * CHUNK, s, NEG)
        p = _softmax_rows(s).astype(BF16)
        o_ref[pl.ds(r0, BAND_ROWS), :] = jnp.dot(p, vb[pl.ds(w0, BAND_WIN), :],
                                                 preferred_element_type=F32).astype(o_ref.dtype)
        return carry

    lax.fori_loop(0, n_groups, group, 0, unroll=True)


def _band_sample_body(hb, dh, q_ref, kc_ref, vc_ref, kn_ref, vn_ref, bc_ref, bn_ref, prev_ref, o_ref):
    del prev_ref
    scale = dh ** -0.5
    for h in range(hb):
        sl = slice(h * dh, (h + 1) * dh)
        q = q_ref[:, sl]
        s1 = lax.dot_general(q, kc_ref[:, h, :].astype(BF16), _NT, preferred_element_type=F32) * scale + bc_ref[h]
        s2 = lax.dot_general(q, kn_ref[:, sl].astype(BF16), _NT, preferred_element_type=F32) * scale + bn_ref[h]
        m = jnp.maximum(jnp.max(s1, axis=-1, keepdims=True), jnp.max(s2, axis=-1, keepdims=True))
        e1 = jnp.exp(s1 - m)
        e2 = jnp.exp(s2 - m)
        den = jnp.sum(e1, axis=-1, keepdims=True) + jnp.sum(e2, axis=-1, keepdims=True)
        o = (jnp.dot((e1 / den).astype(BF16), vc_ref[:, h, :].astype(BF16), preferred_element_type=F32)
             + jnp.dot((e2 / den).astype(BF16), vn_ref[:, sl].astype(BF16), preferred_element_type=F32))
        o_ref[:, sl] = o.astype(o_ref.dtype)


def _rel_bias(table, offset, n_q, n_k):
    p = n_q + n_k
    k = jnp.arange(p)
    diff = jnp.where(k < n_k, k, k - p)
    w = table[:, jnp.clip(offset - diff, -MAX_REL, MAX_REL) + MAX_REL].astype(F32)
    h = table.shape[0]
    flat = jnp.broadcast_to(w[:, None, :], (h, n_q, p)).reshape(h, n_q * p)
    return flat[:, :n_q * (p - 1)].reshape(h, n_q, p - 1)[:, :, :n_k]


def band_attention(q, kv, cache_k, cache_v, table, n_prompt, n_batch):
    rows, d = q.shape
    nh = N_HEADS
    dh = d // nh
    past = PAST_CHUNKS * CHUNK
    band = past + CHUNK
    t_dec = (rows - n_prompt) // n_batch
    cb = cache_k.shape[1]

    lo = (jnp.arange(BAND_ROWS)[:, None] // CHUNK) * CHUNK
    col = jnp.arange(BAND_WIN)[None, :]
    in_band = (col >= lo) & (col < lo + band)
    bias_p = jnp.where(in_band[None], _rel_bias(table, past, BAND_ROWS, BAND_WIN), NEG)
    bias_s = _rel_bias(table, cb, t_dec, cb + t_dec)
    bias_c, bias_n = bias_s[:, :, :cb], bias_s[:, :, cb:]

    bq = min(BAND_Q_ROWS, n_prompt)
    pad_rows = past + n_prompt
    o = pl.pallas_call(
        functools.partial(_band_prompt_body, bq // BAND_ROWS),
        grid=(nh, n_prompt // bq),
        in_specs=[pl.BlockSpec((bq, dh), lambda h, i: (i, h)),
                  pl.BlockSpec((n_prompt, dh), lambda h, i: (0, h)),
                  pl.BlockSpec((n_prompt, dh), lambda h, i: (0, nh + h)),
                  pl.BlockSpec((None, BAND_ROWS, BAND_WIN), lambda h, i: (h, 0, 0))],
        out_specs=pl.BlockSpec((bq, dh), lambda h, i: (i, h)),
        out_shape=jax.ShapeDtypeStruct((rows, d), BF16),
        scratch_shapes=[pltpu.VMEM((pad_rows, dh), BF16), pltpu.VMEM((pad_rows, dh), BF16)],
        compiler_params=_params(("arbitrary", "arbitrary"),
                                4 * n_prompt * dh * 4 + 2 * pad_rows * dh * 2 + 8 * bq * dh * 2),
        name="band_attn_prompt",
    )(q, kv, kv, bias_p)

    hb = min(SAMPLE_HEADS_PER_STEP, nh)
    ngrp = nh // hb
    pb = n_prompt // t_dec
    w = hb * dh
    return pl.pallas_call(
        functools.partial(_band_sample_body, hb, dh),
        grid=(n_batch, ngrp),
        in_specs=[pl.BlockSpec((t_dec, w), lambda b, g: (pb + b, g)),
                  pl.BlockSpec((None, cb, hb, dh), lambda b, g: (b, 0, g, 0)),
                  pl.BlockSpec((None, cb, hb, dh), lambda b, g: (b, 0, g, 0)),
                  pl.BlockSpec((t_dec, w), lambda b, g: (pb + b, g)),
                  pl.BlockSpec((t_dec, w), lambda b, g: (pb + b, ngrp + g)),
                  pl.BlockSpec((hb, t_dec, cb), lambda b, g: (g, 0, 0)),
                  pl.BlockSpec((hb, t_dec, t_dec), lambda b, g: (g, 0, 0)),
                  pl.BlockSpec(memory_space=pl.ANY)],
        out_specs=pl.BlockSpec((t_dec, w), lambda b, g: (pb + b, g)),
        out_shape=jax.ShapeDtypeStruct((rows, d), BF16),
        input_output_aliases={7: 0},
        compiler_params=_params(("arbitrary", "arbitrary"), 4 * cb * w * 4 + 16 * t_dec * w * 4 + 4 * hb * t_dec * cb * 4),
        name="band_attn_sample",
    )(q, cache_k, cache_v, kv, kv, bias_c, bias_n, o)


def _row_copy(src_hbm, dst_ref, sem, r, src_row):
    return pltpu.make_async_copy(src_hbm.at[pl.ds(src_row, 1), :], dst_ref.at[pl.ds(r, 1), :], sem)


def _issue_rows(src_hbm, idx_ref, dst_ref, n_rows, sem):
    def body(b, carry):
        for k in range(GATHER_UNROLL):
            r = b * GATHER_UNROLL + k
            _row_copy(src_hbm, dst_ref, sem, r, idx_ref[0, r]).start(priority=GATHER_DMA_PRIORITY)
        return carry
    lax.fori_loop(0, n_rows // GATHER_UNROLL, body, 0)


def _drain_rows(src_hbm, dst_ref, n_rows, sem):
    def body(b, carry):
        for k in range(GATHER_UNROLL):
            _row_copy(src_hbm, dst_ref, sem, b * GATHER_UNROLL + k, 0).wait()
        return carry
    lax.fori_loop(0, n_rows // GATHER_UNROLL, body, 0)


def _moe_body(item_e, item_blk, item_nsub, slot_cur, slot_next, xn_hbm, wg_ref, wu_ref, wo_ref, o_ref,
              xbuf, sem):
    del item_e, item_blk
    i = pl.program_id(0)
    j = pl.program_id(1)
    n_items = pl.num_programs(0)
    nsub = item_nsub[i]
    slot = i % 2

    @pl.when((nsub > 0) & (j == 0))
    def _():
        @pl.when(i == 0)
        def _():
            _issue_rows(xn_hbm, slot_cur, xbuf.at[0], nsub * MOE_SUB, sem.at[0])

        _drain_rows(xn_hbm, xbuf.at[slot], nsub * MOE_SUB, sem.at[slot])
        nsub_next = item_nsub[jnp.minimum(i + 1, n_items - 1)]

        @pl.when((i + 1 < n_items) & (nsub_next > 0))
        def _():
            _issue_rows(xn_hbm, slot_next, xbuf.at[1 - slot], nsub_next * MOE_SUB, sem.at[1 - slot])

        o_ref[...] = jnp.zeros_like(o_ref)

    @pl.when(nsub > 0)
    def _():
        xcur = xbuf.at[slot]

        def sub(s, carry):
            r0 = pl.multiple_of(s * MOE_SUB, MOE_SUB)
            xb = xcur[pl.ds(r0, MOE_SUB), :]
            g = jnp.dot(xb, wg_ref[...], preferred_element_type=F32)
            u = jnp.dot(xb, wu_ref[...], preferred_element_type=F32)
            h = g * jax.nn.sigmoid(g) * u
            o_ref[pl.ds(r0, MOE_SUB), :] += jnp.dot(h, wo_ref[...], preferred_element_type=F32)
            return carry

        lax.fori_loop(0, nsub, sub, 0)


def moe_experts(xn, slot_tok, w_in, w_out, layer, item_e, item_blk, item_nsub):
    d = xn.shape[1]
    n_items = slot_tok.shape[0]
    n_slots = n_items * MOE_ROWS
    de = w_out.shape[2]
    tj = min(MOE_COL, de)
    nj = de // tj

    def jj(i, j, ie, ib, ns):
        return jnp.where(ns[i] > 0, j, nj - 1)

    grid_spec = pltpu.PrefetchScalarGridSpec(
        num_scalar_prefetch=3,
        grid=(n_items, nj),
        in_specs=[pl.BlockSpec((None, 1, MOE_ROWS), lambda i, j, ie, ib, ns: (ib[i], 0, 0),
                               memory_space=pltpu.SMEM),
                  pl.BlockSpec((None, 1, MOE_ROWS),
                               lambda i, j, ie, ib, ns: (ib[jnp.minimum(i + 1, n_items - 1)], 0, 0),
                               memory_space=pltpu.SMEM),
                  pl.BlockSpec(memory_space=pl.ANY),
                  pl.BlockSpec((None, None, d, tj), lambda i, j, ie, ib, ns: (layer, ie[i], 0, jj(i, j, ie, ib, ns))),
                  pl.BlockSpec((None, None, d, tj),
                               lambda i, j, ie, ib, ns: (layer, ie[i], 0, nj + jj(i, j, ie, ib, ns))),
                  pl.BlockSpec((None, None, tj, d), lambda i, j, ie, ib, ns: (layer, ie[i], jj(i, j, ie, ib, ns), 0))],
        out_specs=pl.BlockSpec((MOE_ROWS, d), lambda i, j, ie, ib, ns: (ib[i], 0)),
        scratch_shapes=[pltpu.VMEM((2, MOE_ROWS, d), F32), pltpu.SemaphoreType.DMA((2,))],
    )
    vmem = 4 * MOE_ROWS * d * 4 + 3 * d * tj * 2 * 4 + 4 * MOE_SUB * d * 4
    return pl.pallas_call(
        _moe_body,
        grid_spec=grid_spec,
        out_shape=jax.ShapeDtypeStruct((n_slots, d), F32),
        compiler_params=_params(("arbitrary", "arbitrary"), vmem),
        name="moe_experts",
    )(item_e, item_blk, item_nsub, slot_tok, slot_tok, xn, w_in, w_in, w_out)


def _route(logits):
    t = logits.shape[0]
    lg = logits[:, :N_GROUPS]
    g_idx = jnp.argmax(lg, axis=-1).astype(jnp.int32)[:, None]
    in_group = lax.broadcasted_iota(jnp.int32, (t, N_GROUPS), 1) == g_idx
    p_group = jnp.sum(jnp.where(in_group, jax.nn.softmax(lg, axis=-1), 0.0), axis=-1, keepdims=True)
    le = logits[:, N_GROUPS:N_GROUPS + N_EXPERTS].reshape(t, N_GROUPS, EXPERTS_PER_GROUP)
    le = jnp.sum(jnp.where(in_group[:, :, None], le, 0.0), axis=1)
    lane = lax.broadcasted_iota(jnp.int32, (t, EXPERTS_PER_GROUP), 1)
    tops_v, tops_j = [], []
    for _ in range(TOP_K):
        j = jnp.argmax(le, axis=-1).astype(jnp.int32)[:, None]
        tops_j.append(j)
        tops_v.append(jnp.max(le, axis=-1, keepdims=True))
        le = jnp.where(lane == j, -jnp.inf, le)
    top_v = jnp.concatenate(tops_v, axis=-1)
    top_j = jnp.concatenate(tops_j, axis=-1)
    gate = p_group * jax.nn.softmax(top_v, axis=-1)
    eidx = g_idx * EXPERTS_PER_GROUP + top_j
    return eidx.astype(jnp.int32), gate


def _dispatch(eidx):
    t = eidx.shape[0]
    a = t * TOP_K
    n_items = a // MOE_ROWS + N_EXPERTS
    flat_e = eidx.reshape(a)
    order = jnp.argsort(flat_e).astype(jnp.int32)
    se = flat_e[order]
    expert_ids = jnp.arange(N_EXPERTS, dtype=jnp.int32)
    counts = jnp.sum((flat_e[:, None] == expert_ids[None, :]).astype(jnp.int32), axis=0)
    nblk = (counts + MOE_ROWS - 1) // MOE_ROWS
    blk_end = jnp.cumsum(nblk)
    blk_start = blk_end - nblk
    start = jnp.cumsum(counts) - counts
    shift = blk_start * MOE_ROWS - start
    dest_sorted = jnp.arange(a, dtype=jnp.int32) + jnp.sum(
        jnp.where(se[:, None] == expert_ids[None, :], shift[None, :], 0), axis=1)
    slot_tok = jnp.zeros((n_items * MOE_ROWS,), jnp.int32).at[dest_sorted].set(order // TOP_K)
    dest = jnp.zeros((a,), jnp.int32).at[order].set(dest_sorted).reshape(t, TOP_K)
    n_used = blk_end[-1]
    item = jnp.arange(n_items, dtype=jnp.int32)
    item_c = jnp.minimum(item, n_used - 1)
    item_e = jnp.minimum(jnp.searchsorted(blk_end, item_c, side='right'), N_EXPERTS - 1).astype(jnp.int32)
    rows_left = counts[item_e] - (item_c - blk_start[item_e]) * MOE_ROWS
    nsub = (jnp.clip(rows_left, 0, MOE_ROWS) + MOE_SUB - 1) // MOE_SUB
    item_nsub = jnp.where(item < n_used, nsub, 0).astype(jnp.int32)
    return slot_tok.reshape(n_items, 1, MOE_ROWS), dest, item_e, item_c.astype(jnp.int32), item_nsub


def _combine_body(n_blocks, n_gain, split_blocks, d_cur, d_next, gate_ref, x_ref, *refs):
    g_refs = refs[:n_gain]
    ys_hbm = refs[n_gain]
    o_refs = refs[n_gain + 1:-2]
    buf, sem = refs[-2:]
    i = pl.program_id(0)
    slot = i % 2
    tb = x_ref.shape[0]

    rows_per_iter = GATHER_UNROLL // TOP_K

    def row_copy(s, k, r, src_row):
        return _row_copy(ys_hbm, buf.at[s, k], sem.at[s], r, src_row)

    def issue(idx_ref, s):
        def body(b, carry):
            for u in range(rows_per_iter):
                r = b * rows_per_iter + u
                for k in range(TOP_K):
                    row_copy(s, k, r, idx_ref[k, r]).start(priority=GATHER_DMA_PRIORITY)
            return carry
        lax.fori_loop(0, tb // rows_per_iter, body, 0)

    @pl.when(i == 0)
    def _():
        issue(d_cur, 0)

    @pl.when(i + 1 < n_blocks)
    def _():
        issue(d_next, 1 - slot)

    def drain(b, carry):
        for u in range(rows_per_iter):
            for k in range(TOP_K):
                row_copy(slot, k, b * rows_per_iter + u, 0).wait()
        return carry

    lax.fori_loop(0, tb // rows_per_iter, drain, 0)
    acc = buf[slot, 0] * gate_ref[:, 0:1]
    for k in range(1, TOP_K):
        acc = acc + buf[slot, k] * gate_ref[:, k:k + 1]
    x = x_ref[...] + acc
    y = x * lax.rsqrt(jnp.mean(x * x, axis=-1, keepdims=True) + RMS_EPS)
    if split_blocks is None:
        o_refs[0][...] = x
        for g_ref, o_ref in zip(g_refs, o_refs[1:]):
            o_ref[...] = (y * g_ref[...]).astype(o_ref.dtype)
    else:
        y = y * g_refs[0][...]

        @pl.when(i < split_blocks)
        def _():
            o_refs[0][...] = y

        @pl.when(i >= split_blocks)
        def _():
            o_refs[1][...] = y


def moe_combine(x, ys, dest, gate, gains, dtypes, n_prompt=None):
    t, d = x.shape
    tb = min(COMBINE_ROWS, t) if n_prompt is None else math.gcd(COMBINE_ROWS, math.gcd(n_prompt, t - n_prompt))
    n_blocks = t // tb
    n_gain = len(gains)
    dest3 = dest.reshape(n_blocks, tb, TOP_K).transpose(0, 2, 1)
    idx = lambda f: pl.BlockSpec((None, TOP_K, tb), lambda i: (f(i), 0, 0), memory_space=pltpu.SMEM)
    row = pl.BlockSpec((tb, d), lambda i: (i, 0))
    vec = pl.BlockSpec((1, d), lambda i: (0, 0))
    if n_prompt is None:
        split = None
        out_specs = [row] * (1 + n_gain)
        out_shape = [jax.ShapeDtypeStruct((t, d), F32)] + [jax.ShapeDtypeStruct((t, d), dt) for dt in dtypes]
    else:
        split = n_prompt // tb
        out_specs = [pl.BlockSpec((tb, d), lambda i: (jnp.minimum(i, split - 1), 0)),
                     pl.BlockSpec((tb, d), lambda i: (jnp.maximum(i - split, 0), 0))]
        out_shape = [jax.ShapeDtypeStruct((n_prompt, d), dtypes[0]),
                     jax.ShapeDtypeStruct((t - n_prompt, d), dtypes[0])]
    return pl.pallas_call(
        functools.partial(_combine_body, n_blocks, n_gain, split),
        grid=(n_blocks,),
        in_specs=[idx(lambda i: i), idx(lambda i: jnp.minimum(i + 1, n_blocks - 1)),
                  pl.BlockSpec((tb, TOP_K), lambda i: (i, 0)), row] + [vec] * n_gain
                 + [pl.BlockSpec(memory_space=pl.ANY)],
        out_specs=out_specs,
        out_shape=out_shape,
        scratch_shapes=[pltpu.VMEM((2, TOP_K, tb, d), F32), pltpu.SemaphoreType.DMA((2,))],
        compiler_params=_params(("arbitrary",), (2 * TOP_K + 8 + 2 * len(out_shape)) * tb * d * 4),
        name="moe_combine",
    )(dest3, dest3, gate, x, *[g.reshape(1, d).astype(F32) for g in gains], ys)


def moe_ffn(x, xn, logits, w_in, w_out, layer, gains, dtypes, n_prompt=None):
    eidx, gate = _route(logits)
    slot_tok, dest, item_e, item_blk, item_nsub = _dispatch(eidx)
    ys = moe_experts(xn, slot_tok, w_in, w_out, layer, item_e, item_blk, item_nsub)
    return moe_combine(x, ys, dest, gate, gains, dtypes, n_prompt)


def _router_weights(w_group, b_group, w_expert, b_expert):
    d = w_group.shape[0]
    pad = LANES - N_GROUPS - N_EXPERTS
    w = jnp.concatenate([w_group, w_expert, jnp.zeros((d, pad), w_group.dtype)], axis=1).astype(BF16)
    b = jnp.concatenate([b_group.astype(F32), b_expert.astype(F32), jnp.zeros((pad,), F32)]).reshape(1, LANES)
    return w, b


def kernel(x_prompt, x_sample, state_ssm_re, state_ssm_im, cache_band_k, cache_band_v, cache_mem_k, cache_mem_v, mem_prompt, norm_mix, norm_mem, norm_memin, norm_ffn, norm_kv, norm_final, ssm_a_re, ssm_a_im, ssm_log_dt, ssm_b_re, ssm_b_im, ssm_c_re, ssm_c_im, ssm_d, ssm_w_glu, w_kv_shared, attn_w_q, attn_rel_bias, attn_w_o, mem_w_q, mem_w_kv, mem_w_o, moe_w_group, moe_b_group, moe_w_expert, moe_b_expert, moe_w_in, moe_w_out):
    bp, seq, d = x_prompt.shape
    nb, t_dec, _ = x_sample.shape
    assert bp == 1
    n_prompt = bp * seq
    depth = norm_mix.shape[0]
    n_a = ssm_a_re.shape[0]
    dh = d // N_HEADS
    mt = mem_prompt.shape[1]
    x = jnp.concatenate([x_prompt.reshape(n_prompt, d), x_sample.reshape(nb * t_dec, d)], axis=0)
    mem = mem_prompt.reshape(mt, d)

    out_re_p, out_im_p, out_re_s, out_im_s, out_mk, out_mv = [], [], [], [], [], []
    def mixer_norms(layer):
        if layer < n_a:
            return [norm_mix[layer]], [F32]
        if layer == n_a:
            return [norm_kv, norm_mix[layer]], [BF16, BF16]
        return [norm_mix[layer]], [BF16]

    kv = None
    normed = rmsnorm(x, *mixer_norms(0))
    for layer in range(depth):
        if layer < n_a:
            a = layer
            (xn,) = normed
            prep = _ssm_prep(ssm_a_re[a], ssm_a_im[a], ssm_log_dt[a], ssm_b_re[a], ssm_b_im[a],
                             ssm_c_re[a], ssm_c_im[a])
            y, re_p, im_p, re_s, im_s = ssm_mix(xn, n_prompt, nb, state_ssm_re[a], state_ssm_im[a], prep)
            out_re_p.append(re_p)
            out_im_p.append(im_p)
            out_re_s.append(re_s)
            out_im_s.append(im_s)
            g = skip_gelu(y, xn, ssm_d[a])
            x = matmul(g, ssm_w_glu, prefix=(a,), n_out=d, glu=True, res=x, name="glu_proj")
        else:
            b = layer - n_a
            if layer == n_a:
                kvn, xn = normed
                kv = matmul(kvn, w_kv_shared, name="kv_proj")
            else:
                (xn,) = normed
            q = matmul(xn, attn_w_q, prefix=(b,), out_dtype=BF16, name="attn_q_proj")
            o = band_attention(q, kv, cache_band_k, cache_band_v, attn_rel_bias[b], n_prompt, nb)
            x = matmul(o, attn_w_o, prefix=(b,), res=x, name="attn_o_proj")

        (memn,) = rmsnorm(mem, [norm_memin[layer]], [BF16])
        kv_mem = matmul(memn, mem_w_kv, prefix=(layer,), name="mem_kv_proj")
        out_mk.append(kv_mem[:, :d].reshape(bp, mt, MEM_HEADS, d // MEM_HEADS))
        out_mv.append(kv_mem[:, d:].reshape(bp, mt, MEM_HEADS, d // MEM_HEADS))
        (xn,) = rmsnorm(x, [norm_mem[layer]], [BF16])
        q = matmul(xn, mem_w_q, prefix=(layer,), out_dtype=BF16, name="mem_q_proj")
        o = mem_attention(q, kv_mem, cache_mem_k, cache_mem_v, layer, n_prompt, nb)
        x = matmul(o, mem_w_o, prefix=(layer,), res=x, name="mem_o_proj")

        w_r, b_r = _router_weights(moe_w_group[layer], moe_b_group[layer], moe_w_expert[layer], moe_b_expert[layer])
        xn, logits = rmsnorm_router(x, norm_ffn[layer], w_r, b_r)
        if layer + 1 < depth:
            x, *normed = moe_ffn(x, xn, logits, moe_w_in, moe_w_out, layer, *mixer_norms(layer + 1))
        else:
            y_p, y_s = moe_ffn(x, xn, logits, moe_w_in, moe_w_out, layer, [norm_final], [F32], n_prompt)

    keep = min(PAST_CHUNKS * CHUNK, seq)
    k_all, v_all = kv[:, :d], kv[:, d:]
    bk_p = k_all[n_prompt - keep:n_prompt].reshape(bp, keep, N_HEADS, dh)
    bv_p = v_all[n_prompt - keep:n_prompt].reshape(bp, keep, N_HEADS, dh)
    keep_s = min(PAST_CHUNKS * CHUNK, t_dec)
    bk_s = k_all[n_prompt:].reshape(nb, t_dec, N_HEADS, dh)[:, t_dec - keep_s:]
    bv_s = v_all[n_prompt:].reshape(nb, t_dec, N_HEADS, dh)[:, t_dec - keep_s:]
    return (y_p.reshape(bp, seq, d), y_s.reshape(nb, t_dec, d),
            jnp.stack(out_re_p), jnp.stack(out_im_p), jnp.stack(out_re_s), jnp.stack(out_im_s),
            bk_p, bv_p, bk_s, bv_s, jnp.stack(out_mk), jnp.stack(out_mv))
```

```python
import functools
import math

import jax
import jax.numpy as jnp
from jax import lax
from jax.experimental import pallas as pl
from jax.experimental.pallas import tpu as pltpu

F32 = jnp.float32
BF16 = jnp.bfloat16

CHUNK = 64
RMS_EPS = 1e-6
SSM_GROUP = 16
SSM_STATE = 64
N_HEADS = 32
PAST_CHUNKS = 8
MAX_REL = 256
MEM_HEADS = 4
N_GROUPS = 8
EXPERTS_PER_GROUP = 8
N_EXPERTS = N_GROUPS * EXPERTS_PER_GROUP
TOP_K = 2

V7X_VMEM_BYTES = 64 * 1024 * 1024
LANES = 128
MXU_DIM = 256

BF16_ROWS = 16
ROW_TILE = 512
MATMUL_ROWS_MAX = 1152
NORM_TILE = 256
COL_TILE = 512
SSM_SUB = 8
SSM_TILE_GROUPS = LANES // SSM_GROUP
BAND_Q_ROWS = 1024
BAND_GROUP = 4
BAND_ROWS = BAND_GROUP * CHUNK
BAND_WIN = (PAST_CHUNKS + BAND_GROUP) * CHUNK
SAMPLE_HEADS_PER_STEP = 8
MOE_ROWS = 384
MOE_SUB = 128
MOE_COL = 256
COMBINE_ROWS = 128
GATHER_UNROLL = 8
GATHER_DMA_PRIORITY = 1
NEG = -1e30


def _params(sem, vmem_bytes):
    limit = min(int(vmem_bytes) + (4 << 20), V7X_VMEM_BYTES - (8 << 20))
    return pltpu.CompilerParams(dimension_semantics=sem, vmem_limit_bytes=limit)


def _rms_body(n_out, x_ref, *refs):
    g_refs, o_refs = refs[:n_out], refs[n_out:]
    x = x_ref[...]
    y = x * lax.rsqrt(jnp.mean(x * x, axis=-1, keepdims=True) + RMS_EPS)
    for g_ref, o_ref in zip(g_refs, o_refs):
        o_ref[...] = (y * g_ref[...]).astype(o_ref.dtype)


def rmsnorm(x, gains, dtypes):
    m, d = x.shape
    tm = min(NORM_TILE, m)
    n = len(gains)
    row = pl.BlockSpec((tm, d), lambda i: (i, 0))
    vec = pl.BlockSpec((1, d), lambda i: (0, 0))
    outs = pl.pallas_call(
        functools.partial(_rms_body, n),
        grid=(m // tm,),
        in_specs=[row] + [vec] * n,
        out_specs=[row] * n,
        out_shape=[jax.ShapeDtypeStruct((m, d), dt) for dt in dtypes],
        compiler_params=_params(("arbitrary",), tm * d * 4 * (2 * (1 + n) + 3)),
        name="rmsnorm",
    )(x, *[g.reshape(1, d).astype(F32) for g in gains])
    return outs


def _rms_router_body(x_ref, g_ref, wr_ref, br_ref, xn_ref, lg_ref):
    x = x_ref[...]
    y = x * lax.rsqrt(jnp.mean(x * x, axis=-1, keepdims=True) + RMS_EPS)
    xn = y * g_ref[...]
    xn_ref[...] = xn
    lg_ref[...] = jnp.dot(xn.astype(BF16), wr_ref[...], preferred_element_type=F32) + br_ref[...]


def rmsnorm_router(x, gain, w_router, b_router):
    m, d = x.shape
    tm = min(NORM_TILE, m)
    nr = w_router.shape[1]
    return pl.pallas_call(
        _rms_router_body,
        grid=(m // tm,),
        in_specs=[pl.BlockSpec((tm, d), lambda i: (i, 0)), pl.BlockSpec((1, d), lambda i: (0, 0)),
                  pl.BlockSpec((d, nr), lambda i: (0, 0)), pl.BlockSpec((1, nr), lambda i: (0, 0))],
        out_specs=[pl.BlockSpec((tm, d), lambda i: (i, 0)), pl.BlockSpec((tm, nr), lambda i: (i, 0))],
        out_shape=[jax.ShapeDtypeStruct((m, d), F32), jax.ShapeDtypeStruct((m, nr), F32)],
        compiler_params=_params(("arbitrary",), tm * d * (2 * 8 + 3 * 4) + 4 * d * nr),
        name="rmsnorm_router",
    )(x, gain.reshape(1, d).astype(F32), w_router, b_router)


def _mm_body(glu, has_res, x_ref, *refs):
    nw = 2 if glu else 1
    w_refs = refs[:nw]
    res_ref = refs[nw] if has_res else None
    o_ref = refs[nw + (1 if has_res else 0)]
    wb_refs = refs[nw + (1 if has_res else 0) + 1:]

    @pl.when(pl.program_id(1) == 0)
    def _():
        for w_ref, wb_ref in zip(w_refs, wb_refs):
            wb_ref[...] = w_ref[...].astype(BF16)

    x = x_ref[...]
    acc = jnp.dot(x, wb_refs[0][...], preferred_element_type=F32)
    if glu:
        gate = jnp.dot(x, wb_refs[1][...], preferred_element_type=F32)
        acc = acc * jax.nn.sigmoid(gate)
    if has_res:
        acc = res_ref[...] + acc
    o_ref[...] = acc.astype(o_ref.dtype)


def _matmul_rows(m):
    best = None
    for t in range(BF16_ROWS, min(m, MATMUL_ROWS_MAX) + 1, BF16_ROWS):
        if m % t == 0:
            best = t
    return best if best is not None else m


def matmul(x, w, prefix=(), n_out=None, glu=False, res=None, out_dtype=F32, name="matmul"):
    m, k = x.shape
    n = n_out if n_out is not None else w.shape[-1]
    tm = _matmul_rows(m)
    tn = min(COL_TILE // 2 if glu else COL_TILE, n)
    lead = (None,) * len(prefix)
    nblk = n // tn
    w_specs = [pl.BlockSpec(lead + (k, tn), lambda j, i: prefix + (0, j))]
    if glu:
        w_specs.append(pl.BlockSpec(lead + (k, tn), lambda j, i: prefix + (0, nblk + j)))
    nw = len(w_specs)
    in_specs = [pl.BlockSpec((tm, k), lambda j, i: (i, 0))] + w_specs
    args = [x] + [w] * nw
    if res is not None:
        in_specs.append(pl.BlockSpec((tm, tn), lambda j, i: (i, j)))
        args.append(res)
    osize = jnp.dtype(out_dtype).itemsize
    vmem = 2 * tm * k * 2 + nw * (2 * k * tn * 4 + k * tn * 2) + 2 * tm * tn * (osize + 4) + 4 * tm * tn * 4
    return pl.pallas_call(
        functools.partial(_mm_body, glu, res is not None),
        grid=(nblk, m // tm),
        in_specs=in_specs,
        out_specs=pl.BlockSpec((tm, tn), lambda j, i: (i, j)),
        out_shape=jax.ShapeDtypeStruct((m, n), out_dtype),
        scratch_shapes=[pltpu.VMEM((k, tn), BF16)] * nw,
        compiler_params=_params(("arbitrary", "arbitrary"), vmem),
        name=name,
    )(*args)


def _ssm_prep(a_re, a_im, log_dt, b_re, b_im, c_re, c_im):
    hi = lax.Precision.HIGHEST
    ls = SSM_SUB
    a_re = a_re.astype(F32)
    a_im = a_im.astype(F32)
    dt = jnp.exp(log_dt.astype(F32))[:, None]
    mag = jnp.exp(a_re * dt)
    ab_re = mag * jnp.cos(a_im * dt)
    ab_im = mag * jnp.sin(a_im * dt)
    den = a_re * a_re + a_im * a_im
    zr = ab_re - 1.0
    f_re = (zr * a_re + ab_im * a_im) / den
    f_im = (ab_im * a_re - zr * a_im) / den
    b_re = b_re.astype(F32)
    b_im = b_im.astype(F32)
    bb_re = f_re[..., None] * b_re - f_im[..., None] * b_im
    bb_im = f_re[..., None] * b_im + f_im[..., None] * b_re
    c_re = c_re.astype(F32)
    c_im = c_im.astype(F32)

    pw_re = [jnp.ones_like(ab_re)]
    pw_im = [jnp.zeros_like(ab_im)]
    for _ in range(ls):
        pr, pi = pw_re[-1], pw_im[-1]
        pw_re.append(pr * ab_re - pi * ab_im)
        pw_im.append(pr * ab_im + pi * ab_re)
    p_re = jnp.stack(pw_re, axis=1)
    p_im = jnp.stack(pw_im, axis=1)
    g = a_re.shape[0]

    cp_re = c_re[:, None] * p_re[:, :, None, :] - c_im[:, None] * p_im[:, :, None, :]
    cp_im = c_re[:, None] * p_im[:, :, None, :] + c_im[:, None] * p_re[:, :, None, :]
    kern = (jnp.einsum('gtpn,gnq->gtpq', cp_re[:, :ls], bb_re, precision=hi)
            - jnp.einsum('gtpn,gnq->gtpq', cp_im[:, :ls], bb_im, precision=hi))
    s_idx = jnp.arange(ls)[:, None]
    t_idx = jnp.arange(ls)[None, :]
    lag = t_idx - s_idx
    toep = kern[:, jnp.clip(lag, 0, ls - 1)]
    toep = jnp.where((lag >= 0)[None, :, :, None, None], toep, 0.0)
    toep = toep.transpose(0, 1, 4, 2, 3).reshape(g, ls * SSM_GROUP, ls * SSM_GROUP)

    pr_rev = p_re[:, ls - 1::-1][:, :, None, :]
    pi_rev = p_im[:, ls - 1::-1][:, :, None, :]
    bbt_re = bb_re.transpose(0, 2, 1)[:, None]
    bbt_im = bb_im.transpose(0, 2, 1)[:, None]
    win_re = pr_rev * bbt_re - pi_rev * bbt_im
    win_im = pr_rev * bbt_im + pi_rev * bbt_re
    win = jnp.concatenate([win_re, win_im], axis=-1).reshape(g, ls * SSM_GROUP, 2 * SSM_STATE)

    wo_re = cp_re[:, 1:].transpose(0, 3, 1, 2).reshape(g, SSM_STATE, ls * SSM_GROUP)
    wo_im = -cp_im[:, 1:].transpose(0, 3, 1, 2).reshape(g, SSM_STATE, ls * SSM_GROUP)
    wout = jnp.concatenate([wo_re, wo_im], axis=1)

    tg = SSM_TILE_GROUPS
    nt = g // tg
    lp = ls * SSM_GROUP
    c_t = toep.astype(BF16).reshape(nt, tg, ls, SSM_GROUP, lp).transpose(0, 2, 1, 3, 4).reshape(nt, ls * LANES, lp)
    c_win = win.astype(BF16).reshape(nt, tg, ls, SSM_GROUP, 2 * SSM_STATE).transpose(0, 2, 1, 3, 4).reshape(
        nt, ls * LANES, 2 * SSM_STATE)
    c_out = wout.astype(BF16).reshape(nt, tg, 2, SSM_STATE, lp).transpose(0, 2, 1, 3, 4).reshape(
        nt, 2 * tg * SSM_STATE, lp)
    al_re = p_re[:, ls].reshape(nt, tg * SSM_STATE)
    al_im = p_im[:, ls].reshape(nt, tg * SSM_STATE)
    a_rows = jnp.stack([jnp.concatenate([al_re, al_re], -1), jnp.concatenate([-al_im, al_im], -1)], axis=1)
    return c_t, c_win, c_out, a_rows


def _spread_groups(c, row_unit, col_unit):
    tg = SSM_TILE_GROUPS
    r, w = c.shape
    wide = w * tg
    src = lax.broadcasted_iota(jnp.int32, (w, wide), 0)
    dst = lax.broadcasted_iota(jnp.int32, (w, wide), 1)
    sel = ((src // col_unit == dst // (tg * col_unit)) & (src % col_unit == dst % col_unit)).astype(BF16)
    rep = jnp.dot(c, sel, preferred_element_type=F32)
    row_g = (lax.broadcasted_iota(jnp.int32, (r, wide), 0) // row_unit) % tg
    col_g = (lax.broadcasted_iota(jnp.int32, (r, wide), 1) // col_unit) % tg
    return jnp.where(row_g == col_g, rep, 0.0).astype(BF16)


def _ssm_body(n_prompt, nb, t_dec, xn_ref, ct_ref, cwin_ref, cout_ref, a_ref, h0_ref,
              y_ref, hp_ref, hs_ref, u_s, v_s, hin_s, t_ref, win_ref, wout_ref):
    ls = SSM_SUB
    rp = n_prompt // ls
    ns = t_dec // ls
    half = v_s.shape[1] // 2
    t_ref[...] = _spread_groups(ct_ref[...], SSM_GROUP, SSM_GROUP)
    win_ref[...] = _spread_groups(cwin_ref[...], SSM_GROUP, SSM_STATE)
    wout_ref[...] = _spread_groups(cout_ref[...], SSM_STATE, SSM_GROUP)

    def sub_rows(ref, s, j):
        if j is None:
            return ref.at[pl.ds(s, rp, stride=ls), :]
        return ref.at[pl.ds(n_prompt + j * ls + s, nb, stride=t_dec), :]

    for s in range(ls):
        cols = slice(s * LANES, (s + 1) * LANES)
        u_s[0:rp, cols] = sub_rows(xn_ref, s, None)[...].astype(BF16)
        for j in range(ns):
            u_s[rp + j * nb:rp + (j + 1) * nb, cols] = sub_rows(xn_ref, s, j)[...].astype(BF16)

    v_s[...] = jnp.dot(u_s[...], win_ref[...], preferred_element_type=F32)
    a1 = a_ref[0:1, :]
    a2 = a_ref[1:2, :]

    def step(h, v):
        h_sw = jnp.concatenate([h[:, half:], h[:, :half]], axis=1)
        return h * a1 + h_sw * a2 + v

    def body(c, h):
        hin_s[pl.ds(c, 1), :] = h
        return step(h, v_s[pl.ds(c, 1), :])

    h = lax.fori_loop(0, rp, body, jnp.zeros((1, 2 * half), F32), unroll=4)
    hp_ref[...] = h
    h = h0_ref[...]
    for j in range(ns):
        r0 = rp + j * nb
        hin_s[r0:r0 + nb, :] = h
        h = step(h, v_s[r0:r0 + nb, :])
    hs_ref[...] = h

    hin = hin_s[...].astype(BF16)
    for n0 in range(0, ls * LANES, MXU_DIM):
        k_hi = n0 + MXU_DIM
        ycol = (jnp.dot(u_s[:, :k_hi], t_ref[:k_hi, n0:n0 + MXU_DIM], preferred_element_type=F32)
                + jnp.dot(hin, wout_ref[:, n0:n0 + MXU_DIM], preferred_element_type=F32))
        for t in range(n0 // LANES, (n0 + MXU_DIM) // LANES):
            piece = ycol[:, t * LANES - n0:(t + 1) * LANES - n0]
            sub_rows(y_ref, t, None)[...] = piece[0:rp]
            for j in range(ns):
                sub_rows(y_ref, t, j)[...] = piece[rp + j * nb:rp + (j + 1) * nb]


def ssm_mix(xn, n_prompt, n_batch, h0_re, h0_im, prep):
    rows, d = xn.shape
    g = d // SSM_GROUP
    ls = SSM_SUB
    tg = SSM_TILE_GROUPS
    nt = g // tg
    t_dec = (rows - n_prompt) // n_batch
    r = n_prompt // ls + (t_dec // ls) * n_batch
    kw = ls * LANES
    sw = 2 * tg * SSM_STATE
    c_t, c_win, c_out, a_rows = prep
    h0 = jnp.concatenate([h0_re.astype(F32).reshape(n_batch, nt, sw // 2),
                          h0_im.astype(F32).reshape(n_batch, nt, sw // 2)], axis=-1).transpose(1, 0, 2)

    mat = lambda a, b: pl.BlockSpec((None, a, b), lambda i: (i, 0, 0))
    col = pl.BlockSpec((rows, LANES), lambda i: (0, i))
    vmem = (4 * rows * LANES * 4 + (kw * kw + 2 * kw * sw) * 2 + r * kw * 2 + 2 * r * sw * 4
            + r * sw * 4 + 2 * kw * max(kw, sw) * 4)
    y, hp, hs = pl.pallas_call(
        functools.partial(_ssm_body, n_prompt, n_batch, t_dec),
        grid=(nt,),
        in_specs=[col, mat(kw, c_t.shape[2]), mat(kw, c_win.shape[2]), mat(sw, c_out.shape[2]),
                  mat(2, sw), mat(n_batch, sw)],
        out_specs=[col, mat(1, sw), mat(n_batch, sw)],
        out_shape=[jax.ShapeDtypeStruct((rows, d), F32), jax.ShapeDtypeStruct((nt, 1, sw), F32),
                   jax.ShapeDtypeStruct((nt, n_batch, sw), F32)],
        scratch_shapes=[pltpu.VMEM((r, kw), BF16), pltpu.VMEM((r, sw), F32), pltpu.VMEM((r, sw), F32),
                        pltpu.VMEM((kw, kw), BF16), pltpu.VMEM((kw, sw), BF16), pltpu.VMEM((sw, kw), BF16)],
        compiler_params=_params(("arbitrary",), vmem),
        name="ssm",
    )(xn, c_t, c_win, c_out, a_rows, h0)

    half = sw // 2
    re_p = hp[:, 0, :half].reshape(g, SSM_STATE)[None]
    im_p = hp[:, 0, half:].reshape(g, SSM_STATE)[None]
    re_s = hs[:, :, :half].transpose(1, 0, 2).reshape(n_batch, g, SSM_STATE)
    im_s = hs[:, :, half:].transpose(1, 0, 2).reshape(n_batch, g, SSM_STATE)
    return y, re_p, im_p, re_s, im_s


def _gelu_body(y_ref, xn_ref, d_ref, o_ref):
    y = y_ref[...] + d_ref[...] * xn_ref[...]
    c = math.sqrt(2.0 / math.pi)
    cdf = 0.5 * (1.0 + jnp.tanh(c * (y + 0.044715 * (y * y * y))))
    o_ref[...] = (y * cdf).astype(o_ref.dtype)


def skip_gelu(y, xn, d_skip):
    m, d = y.shape
    tm = min(NORM_TILE, m)
    row = pl.BlockSpec((tm, d), lambda i: (i, 0))
    return pl.pallas_call(
        _gelu_body,
        grid=(m // tm,),
        in_specs=[row, row, pl.BlockSpec((1, d), lambda i: (0, 0))],
        out_specs=row,
        out_shape=jax.ShapeDtypeStruct((m, d), BF16),
        compiler_params=_params(("arbitrary",), tm * d * (2 * 10 + 4 * 4)),
        name="skip_gelu",
    )(y, xn, d_skip.reshape(1, d).astype(F32))


_NT = (((1,), (1,)), ((), ()))


def _softmax_rows(s):
    m = jnp.max(s, axis=-1, keepdims=True)
    e = jnp.exp(s - m)
    return e / jnp.sum(e, axis=-1, keepdims=True)


def _memattn_body(q_ref, k_ref, v_ref, o_ref, kb, vb):
    @pl.when(pl.program_id(0) == 0)
    def _():
        kb[...] = k_ref[...].astype(BF16)
        vb[...] = v_ref[...].astype(BF16)

    dh = q_ref.shape[1] // MEM_HEADS
    scale = dh ** -0.5
    for h in range(MEM_HEADS):
        sl = slice(h * dh, (h + 1) * dh)
        s = lax.dot_general(q_ref[:, sl], kb[:, sl], _NT, preferred_element_type=F32) * scale
        p = _softmax_rows(s).astype(BF16)
        o_ref[:, sl] = jnp.dot(p, vb[:, sl], preferred_element_type=F32).astype(o_ref.dtype)


def _memattn_cache_body(q_ref, k_ref, v_ref, prev_ref, o_ref):
    del prev_ref
    n_heads, dh = k_ref.shape[1], k_ref.shape[2]
    scale = dh ** -0.5
    t = q_ref.shape[0]
    heads = [slice(h * dh, (h + 1) * dh) for h in range(n_heads)]
    q = q_ref[...].astype(F32)
    s = jnp.concatenate([lax.dot_general(q[:, sl], k_ref[:, h, :], _NT, preferred_element_type=F32)
                         for h, sl in enumerate(heads)], axis=0)
    p = _softmax_rows(s * scale)
    for h, sl in enumerate(heads):
        o_ref[:, sl] = jnp.dot(p[h * t:(h + 1) * t], v_ref[:, h, :],
                               preferred_element_type=F32).astype(o_ref.dtype)


def mem_attention(q, kv_prompt, cache_k, cache_v, layer, n_prompt, n_batch):
    rows, d = q.shape
    mt = kv_prompt.shape[0]
    tm = min(ROW_TILE, n_prompt)
    t_dec = (rows - n_prompt) // n_batch
    scratch = [pltpu.VMEM((mt, d), BF16), pltpu.VMEM((mt, d), BF16)]
    vmem_kv = 2 * 2 * mt * d * 4 + 2 * mt * d * 2
    o = pl.pallas_call(
        _memattn_body,
        grid=(n_prompt // tm,),
        in_specs=[pl.BlockSpec((tm, d), lambda i: (i, 0)),
                  pl.BlockSpec((mt, d), lambda i: (0, 0)),
                  pl.BlockSpec((mt, d), lambda i: (0, 1))],
        out_specs=pl.BlockSpec((tm, d), lambda i: (i, 0)),
        out_shape=jax.ShapeDtypeStruct((rows, d), BF16),
        scratch_shapes=scratch,
        compiler_params=_params(("arbitrary",), vmem_kv + 4 * tm * d * 2 + 2 * tm * d * 4),
        name="mem_attn_prompt",
    )(q, kv_prompt, kv_prompt)
    pb = n_prompt // t_dec
    dh = d // MEM_HEADS
    slab = pl.BlockSpec((None, None, mt, MEM_HEADS, dh), lambda b: (layer, b, 0, 0, 0))
    return pl.pallas_call(
        _memattn_cache_body,
        grid=(n_batch,),
        in_specs=[pl.BlockSpec((t_dec, d), lambda b: (pb + b, 0)), slab, slab,
                  pl.BlockSpec(memory_space=pl.ANY)],
        out_specs=pl.BlockSpec((t_dec, d), lambda b: (pb + b, 0)),
        out_shape=jax.ShapeDtypeStruct((rows, d), BF16),
        input_output_aliases={3: 0},
        compiler_params=_params(("arbitrary",), 2 * 2 * 2 * mt * d * 4 + 8 * t_dec * d * 4),
        name="mem_attn_sample",
    )(q, cache_k, cache_v, o)


def _band_prompt_body(n_groups, q_ref, k_ref, v_ref, b_ref, o_ref, kb, vb):
    qb = pl.program_id(1)
    s_len, dh = k_ref.shape
    past = PAST_CHUNKS * CHUNK

    @pl.when(qb == 0)
    def _():
        for ref, src in ((kb, k_ref), (vb, v_ref)):
            ref[0:past, :] = jnp.zeros((past, dh), BF16)
            ref[past:past + s_len, :] = src[...].astype(BF16)

    bias = b_ref[...]
    col = lax.broadcasted_iota(jnp.int32, (BAND_ROWS, BAND_WIN), 1)
    scale = dh ** -0.5

    def group(gi, carry):
        c0 = (qb * n_groups + gi) * BAND_GROUP
        r0 = pl.multiple_of(gi * BAND_ROWS, BAND_ROWS)
        w0 = pl.multiple_of(c0 * CHUNK, BAND_ROWS)
        q = q_ref[pl.ds(r0, BAND_ROWS), :]
        s = lax.dot_general(q, kb[pl.ds(w0, BAND_WIN), :], _NT, preferred_element_type=F32) * scale + bias
        s = jnp.where(col >= (PAST_CHUNKS - c0) * CHUNK, s, NEG)
        p = _softmax_rows(s).astype(BF16)
        o_ref[pl.ds(r0, BAND_ROWS), :] = jnp.dot(p, vb[pl.ds(w0, BAND_WIN), :],
                                                 preferred_element_type=F32).astype(o_ref.dtype)
        return carry

    lax.fori_loop(0, n_groups, group, 0, unroll=True)


def _band_sample_body(hb, dh, q_ref, kc_ref, vc_ref, kn_ref, vn_ref, bc_ref, bn_ref, prev_ref, o_ref):
    del prev_ref
    scale = dh ** -0.5
    t = q_ref.shape[0]
    heads = [slice(h * dh, (h + 1) * dh) for h in range(hb)]
    q = q_ref[...].astype(F32)
    s1 = jnp.concatenate([lax.dot_general(q[:, sl], kc_ref[:, h, :], _NT,
                                          preferred_element_type=F32) * scale + bc_ref[h]
                          for h, sl in enumerate(heads)], axis=0)
    s2 = jnp.concatenate([lax.dot_general(q[:, sl], kn_ref[:, sl], _NT,
                                          preferred_element_type=F32) * scale + bn_ref[h]
                          for h, sl in enumerate(heads)], axis=0)
    m = jnp.maximum(jnp.max(s1, axis=-1, keepdims=True), jnp.max(s2, axis=-1, keepdims=True))
    e1 = jnp.exp(s1 - m)
    e2 = jnp.exp(s2 - m)
    den = jnp.sum(e1, axis=-1, keepdims=True) + jnp.sum(e2, axis=-1, keepdims=True)
    p1 = e1 / den
    p2 = e2 / den
    for h, sl in enumerate(heads):
        rows = slice(h * t, (h + 1) * t)
        o = (jnp.dot(p1[rows], vc_ref[:, h, :], preferred_element_type=F32)
             + jnp.dot(p2[rows], vn_ref[:, sl], preferred_element_type=F32))
        o_ref[:, sl] = o.astype(o_ref.dtype)


def _rel_bias(table, offset, n_q, n_k):
    p = n_q + n_k
    k = jnp.arange(p)
    diff = jnp.where(k < n_k, k, k - p)
    w = table[:, jnp.clip(offset - diff, -MAX_REL, MAX_REL) + MAX_REL].astype(F32)
    h = table.shape[0]
    flat = jnp.broadcast_to(w[:, None, :], (h, n_q, p)).reshape(h, n_q * p)
    return flat[:, :n_q * (p - 1)].reshape(h, n_q, p - 1)[:, :, :n_k]


def band_attention(q, kv, cache_k, cache_v, table, n_prompt, n_batch):
    rows, d = q.shape
    nh = N_HEADS
    dh = d // nh
    past = PAST_CHUNKS * CHUNK
    band = past + CHUNK
    t_dec = (rows - n_prompt) // n_batch
    cb = cache_k.shape[1]

    lo = (jnp.arange(BAND_ROWS)[:, None] // CHUNK) * CHUNK
    col = jnp.arange(BAND_WIN)[None, :]
    in_band = (col >= lo) & (col < lo + band)
    bias_p = jnp.where(in_band[None], _rel_bias(table, past, BAND_ROWS, BAND_WIN), NEG)
    bias_s = _rel_bias(table, cb, t_dec, cb + t_dec)
    bias_c, bias_n = bias_s[:, :, :cb], bias_s[:, :, cb:]

    bq = min(BAND_Q_ROWS, n_prompt)
    pad_rows = past + n_prompt
    o = pl.pallas_call(
        functools.partial(_band_prompt_body, bq // BAND_ROWS),
        grid=(nh, n_prompt // bq),
        in_specs=[pl.BlockSpec((bq, dh), lambda h, i: (i, h)),
                  pl.BlockSpec((n_prompt, dh), lambda h, i: (0, h)),
                  pl.BlockSpec((n_prompt, dh), lambda h, i: (0, nh + h)),
                  pl.BlockSpec((None, BAND_ROWS, BAND_WIN), lambda h, i: (h, 0, 0))],
        out_specs=pl.BlockSpec((bq, dh), lambda h, i: (i, h)),
        out_shape=jax.ShapeDtypeStruct((rows, d), BF16),
        scratch_shapes=[pltpu.VMEM((pad_rows, dh), BF16), pltpu.VMEM((pad_rows, dh), BF16)],
        compiler_params=_params(("arbitrary", "arbitrary"),
                                4 * n_prompt * dh * 4 + 2 * pad_rows * dh * 2 + 8 * bq * dh * 2),
        name="band_attn_prompt",
    )(q, kv, kv, bias_p)

    hb = min(SAMPLE_HEADS_PER_STEP, nh)
    ngrp = nh // hb
    pb = n_prompt // t_dec
    w = hb * dh
    return pl.pallas_call(
        functools.partial(_band_sample_body, hb, dh),
        grid=(n_batch, ngrp),
        in_specs=[pl.BlockSpec((t_dec, w), lambda b, g: (pb + b, g)),
                  pl.BlockSpec((None, cb, hb, dh), lambda b, g: (b, 0, g, 0)),
                  pl.BlockSpec((None, cb, hb, dh), lambda b, g: (b, 0, g, 0)),
                  pl.BlockSpec((t_dec, w), lambda b, g: (pb + b, g)),
                  pl.BlockSpec((t_dec, w), lambda b, g: (pb + b, ngrp + g)),
                  pl.BlockSpec((hb, t_dec, cb), lambda b, g: (g, 0, 0)),
                  pl.BlockSpec((hb, t_dec, t_dec), lambda b, g: (g, 0, 0)),
                  pl.BlockSpec(memory_space=pl.ANY)],
        out_specs=pl.BlockSpec((t_dec, w), lambda b, g: (pb + b, g)),
        out_shape=jax.ShapeDtypeStruct((rows, d), BF16),
        input_output_aliases={7: 0},
        compiler_params=_params(("arbitrary", "arbitrary"), 4 * cb * w * 4 + 16 * t_dec * w * 4 + 4 * hb * t_dec * cb * 4),
        name="band_attn_sample",
    )(q, cache_k, cache_v, kv, kv, bias_c, bias_n, o)


def _row_copy(src_hbm, dst_ref, sem, r, src_row):
    return pltpu.make_async_copy(src_hbm.at[pl.ds(src_row, 1), :], dst_ref.at[pl.ds(r, 1), :], sem)


def _issue_rows(src_hbm, idx_ref, dst_ref, n_rows, sem):
    def body(b, carry):
        for k in range(GATHER_UNROLL):
            r = b * GATHER_UNROLL + k
            _row_copy(src_hbm, dst_ref, sem, r, idx_ref[0, r]).start(priority=GATHER_DMA_PRIORITY)
        return carry
    lax.fori_loop(0, n_rows // GATHER_UNROLL, body, 0)


def _drain_rows(src_hbm, dst_ref, n_rows, sem):
    def body(b, carry):
        for k in range(GATHER_UNROLL):
            _row_copy(src_hbm, dst_ref, sem, b * GATHER_UNROLL + k, 0).wait()
        return carry
    lax.fori_loop(0, n_rows // GATHER_UNROLL, body, 0)


def _moe_body(item_e, item_blk, item_nsub, slot_cur, slot_next, xn_hbm, wg_ref, wu_ref, wo_ref, o_ref,
              xbuf, sem):
    del item_e, item_blk
    i = pl.program_id(0)
    j = pl.program_id(1)
    n_items = pl.num_programs(0)
    nsub = item_nsub[i]
    slot = i % 2

    @pl.when((nsub > 0) & (j == 0))
    def _():
        @pl.when(i == 0)
        def _():
            _issue_rows(xn_hbm, slot_cur, xbuf.at[0], nsub * MOE_SUB, sem.at[0])

        _drain_rows(xn_hbm, xbuf.at[slot], nsub * MOE_SUB, sem.at[slot])
        nsub_next = item_nsub[jnp.minimum(i + 1, n_items - 1)]

        @pl.when((i + 1 < n_items) & (nsub_next > 0))
        def _():
            _issue_rows(xn_hbm, slot_next, xbuf.at[1 - slot], nsub_next * MOE_SUB, sem.at[1 - slot])

        o_ref[...] = jnp.zeros_like(o_ref)

    @pl.when(nsub > 0)
    def _():
        xcur = xbuf.at[slot]

        def sub(s, carry):
            r0 = pl.multiple_of(s * MOE_SUB, MOE_SUB)
            xb = xcur[pl.ds(r0, MOE_SUB), :]
            g = jnp.dot(xb, wg_ref[...], preferred_element_type=F32)
            u = jnp.dot(xb, wu_ref[...], preferred_element_type=F32)
            h = g * jax.nn.sigmoid(g) * u
            o_ref[pl.ds(r0, MOE_SUB), :] += jnp.dot(h, wo_ref[...], preferred_element_type=F32)
            return carry

        lax.fori_loop(0, nsub, sub, 0)


def moe_experts(xn, slot_tok, w_in, w_out, layer, item_e, item_blk, item_nsub):
    d = xn.shape[1]
    n_items = slot_tok.shape[0]
    n_slots = n_items * MOE_ROWS
    de = w_out.shape[2]
    tj = min(MOE_COL, de)
    nj = de // tj

    def jj(i, j, ie, ib, ns):
        return jnp.where(ns[i] > 0, j, nj - 1)

    grid_spec = pltpu.PrefetchScalarGridSpec(
        num_scalar_prefetch=3,
        grid=(n_items, nj),
        in_specs=[pl.BlockSpec((None, 1, MOE_ROWS), lambda i, j, ie, ib, ns: (ib[i], 0, 0),
                               memory_space=pltpu.SMEM),
                  pl.BlockSpec((None, 1, MOE_ROWS),
                               lambda i, j, ie, ib, ns: (ib[jnp.minimum(i + 1, n_items - 1)], 0, 0),
                               memory_space=pltpu.SMEM),
                  pl.BlockSpec(memory_space=pl.ANY),
                  pl.BlockSpec((None, None, d, tj), lambda i, j, ie, ib, ns: (layer, ie[i], 0, jj(i, j, ie, ib, ns))),
                  pl.BlockSpec((None, None, d, tj),
                               lambda i, j, ie, ib, ns: (layer, ie[i], 0, nj + jj(i, j, ie, ib, ns))),
                  pl.BlockSpec((None, None, tj, d), lambda i, j, ie, ib, ns: (layer, ie[i], jj(i, j, ie, ib, ns), 0))],
        out_specs=pl.BlockSpec((MOE_ROWS, d), lambda i, j, ie, ib, ns: (ib[i], 0)),
        scratch_shapes=[pltpu.VMEM((2, MOE_ROWS, d), F32), pltpu.SemaphoreType.DMA((2,))],
    )
    vmem = 4 * MOE_ROWS * d * 4 + 3 * d * tj * 2 * 4 + 4 * MOE_SUB * d * 4
    return pl.pallas_call(
        _moe_body,
        grid_spec=grid_spec,
        out_shape=jax.ShapeDtypeStruct((n_slots, d), F32),
        compiler_params=_params(("arbitrary", "arbitrary"), vmem),
        name="moe_experts",
    )(item_e, item_blk, item_nsub, slot_tok, slot_tok, xn, w_in, w_in, w_out)


def _route(logits):
    t = logits.shape[0]
    lg = logits[:, :N_GROUPS]
    g_idx = jnp.argmax(lg, axis=-1).astype(jnp.int32)[:, None]
    in_group = lax.broadcasted_iota(jnp.int32, (t, N_GROUPS), 1) == g_idx
    p_group = jnp.sum(jnp.where(in_group, jax.nn.softmax(lg, axis=-1), 0.0), axis=-1, keepdims=True)
    le = logits[:, N_GROUPS:N_GROUPS + N_EXPERTS].reshape(t, N_GROUPS, EXPERTS_PER_GROUP)
    le = jnp.sum(jnp.where(in_group[:, :, None], le, 0.0), axis=1)
    lane = lax.broadcasted_iota(jnp.int32, (t, EXPERTS_PER_GROUP), 1)
    tops_v, tops_j = [], []
    for _ in range(TOP_K):
        j = jnp.argmax(le, axis=-1).astype(jnp.int32)[:, None]
        tops_j.append(j)
        tops_v.append(jnp.max(le, axis=-1, keepdims=True))
        le = jnp.where(lane == j, -jnp.inf, le)
    top_v = jnp.concatenate(tops_v, axis=-1)
    top_j = jnp.concatenate(tops_j, axis=-1)
    gate = p_group * jax.nn.softmax(top_v, axis=-1)
    eidx = g_idx * EXPERTS_PER_GROUP + top_j
    return eidx.astype(jnp.int32), gate


def _dispatch(eidx):
    t = eidx.shape[0]
    a = t * TOP_K
    n_items = a // MOE_ROWS + N_EXPERTS
    flat_e = eidx.reshape(a)
    order = jnp.argsort(flat_e).astype(jnp.int32)
    se = flat_e[order]
    expert_ids = jnp.arange(N_EXPERTS, dtype=jnp.int32)
    counts = jnp.sum((flat_e[:, None] == expert_ids[None, :]).astype(jnp.int32), axis=0)
    nblk = (counts + MOE_ROWS - 1) // MOE_ROWS
    blk_end = jnp.cumsum(nblk)
    blk_start = blk_end - nblk
    start = jnp.cumsum(counts) - counts
    shift = blk_start * MOE_ROWS - start
    dest_sorted = jnp.arange(a, dtype=jnp.int32) + jnp.sum(
        jnp.where(se[:, None] == expert_ids[None, :], shift[None, :], 0), axis=1)
    slot_tok = jnp.zeros((n_items * MOE_ROWS,), jnp.int32).at[dest_sorted].set(order // TOP_K)
    dest = jnp.zeros((a,), jnp.int32).at[order].set(dest_sorted).reshape(t, TOP_K)
    n_used = blk_end[-1]
    item = jnp.arange(n_items, dtype=jnp.int32)
    item_c = jnp.minimum(item, n_used - 1)
    item_e = jnp.minimum(jnp.searchsorted(blk_end, item_c, side='right'), N_EXPERTS - 1).astype(jnp.int32)
    rows_left = counts[item_e] - (item_c - blk_start[item_e]) * MOE_ROWS
    nsub = (jnp.clip(rows_left, 0, MOE_ROWS) + MOE_SUB - 1) // MOE_SUB
    item_nsub = jnp.where(item < n_used, nsub, 0).astype(jnp.int32)
    return slot_tok.reshape(n_items, 1, MOE_ROWS), dest, item_e, item_c.astype(jnp.int32), item_nsub


def _combine_body(n_blocks, n_gain, split_blocks, d_cur, d_next, gate_ref, x_ref, *refs):
    g_refs = refs[:n_gain]
    ys_hbm = refs[n_gain]
    o_refs = refs[n_gain + 1:-2]
    buf, sem = refs[-2:]
    i = pl.program_id(0)
    slot = i % 2
    tb = x_ref.shape[0]

    rows_per_iter = GATHER_UNROLL // TOP_K

    def row_copy(s, k, r, src_row):
        return _row_copy(ys_hbm, buf.at[s, k], sem.at[s], r, src_row)

    def issue(idx_ref, s):
        def body(b, carry):
            for u in range(rows_per_iter):
                r = b * rows_per_iter + u
                for k in range(TOP_K):
                    row_copy(s, k, r, idx_ref[k, r]).start(priority=GATHER_DMA_PRIORITY)
            return carry
        lax.fori_loop(0, tb // rows_per_iter, body, 0)

    @pl.when(i == 0)
    def _():
        issue(d_cur, 0)

    @pl.when(i + 1 < n_blocks)
    def _():
        issue(d_next, 1 - slot)

    def drain(b, carry):
        for u in range(rows_per_iter):
            for k in range(TOP_K):
                row_copy(slot, k, b * rows_per_iter + u, 0).wait()
        return carry

    lax.fori_loop(0, tb // rows_per_iter, drain, 0)
    acc = buf[slot, 0] * gate_ref[:, 0:1]
    for k in range(1, TOP_K):
        acc = acc + buf[slot, k] * gate_ref[:, k:k + 1]
    x = x_ref[...] + acc
    y = x * lax.rsqrt(jnp.mean(x * x, axis=-1, keepdims=True) + RMS_EPS)
    if split_blocks is None:
        o_refs[0][...] = x
        for g_ref, o_ref in zip(g_refs, o_refs[1:]):
            o_ref[...] = (y * g_ref[...]).astype(o_ref.dtype)
    else:
        y = y * g_refs[0][...]

        @pl.when(i < split_blocks)
        def _():
            o_refs[0][...] = y

        @pl.when(i >= split_blocks)
        def _():
            o_refs[1][...] = y


def moe_combine(x, ys, dest, gate, gains, dtypes, n_prompt=None):
    t, d = x.shape
    tb = min(COMBINE_ROWS, t) if n_prompt is None else math.gcd(COMBINE_ROWS, math.gcd(n_prompt, t - n_prompt))
    n_blocks = t // tb
    n_gain = len(gains)
    dest3 = dest.reshape(n_blocks, tb, TOP_K).transpose(0, 2, 1)
    idx = lambda f: pl.BlockSpec((None, TOP_K, tb), lambda i: (f(i), 0, 0), memory_space=pltpu.SMEM)
    row = pl.BlockSpec((tb, d), lambda i: (i, 0))
    vec = pl.BlockSpec((1, d), lambda i: (0, 0))
    if n_prompt is None:
        split = None
        out_specs = [row] * (1 + n_gain)
        out_shape = [jax.ShapeDtypeStruct((t, d), F32)] + [jax.ShapeDtypeStruct((t, d), dt) for dt in dtypes]
    else:
        split = n_prompt // tb
        out_specs = [pl.BlockSpec((tb, d), lambda i: (jnp.minimum(i, split - 1), 0)),
                     pl.BlockSpec((tb, d), lambda i: (jnp.maximum(i - split, 0), 0))]
        out_shape = [jax.ShapeDtypeStruct((n_prompt, d), dtypes[0]),
                     jax.ShapeDtypeStruct((t - n_prompt, d), dtypes[0])]
    return pl.pallas_call(
        functools.partial(_combine_body, n_blocks, n_gain, split),
        grid=(n_blocks,),
        in_specs=[idx(lambda i: i), idx(lambda i: jnp.minimum(i + 1, n_blocks - 1)),
                  pl.BlockSpec((tb, TOP_K), lambda i: (i, 0)), row] + [vec] * n_gain
                 + [pl.BlockSpec(memory_space=pl.ANY)],
        out_specs=out_specs,
        out_shape=out_shape,
        scratch_shapes=[pltpu.VMEM((2, TOP_K, tb, d), F32), pltpu.SemaphoreType.DMA((2,))],
        compiler_params=_params(("arbitrary",), (2 * TOP_K + 8 + 2 * len(out_shape)) * tb * d * 4),
        name="moe_combine",
    )(dest3, dest3, gate, x, *[g.reshape(1, d).astype(F32) for g in gains], ys)


def moe_ffn(x, xn, logits, w_in, w_out, layer, gains, dtypes, n_prompt=None):
    eidx, gate = _route(logits)
    slot_tok, dest, item_e, item_blk, item_nsub = _dispatch(eidx)
    ys = moe_experts(xn, slot_tok, w_in, w_out, layer, item_e, item_blk, item_nsub)
    return moe_combine(x, ys, dest, gate, gains, dtypes, n_prompt)


def _router_weights(w_group, b_group, w_expert, b_expert):
    d = w_group.shape[0]
    pad = LANES - N_GROUPS - N_EXPERTS
    w = jnp.concatenate([w_group, w_expert, jnp.zeros((d, pad), w_group.dtype)], axis=1).astype(BF16)
    b = jnp.concatenate([b_group.astype(F32), b_expert.astype(F32), jnp.zeros((pad,), F32)]).reshape(1, LANES)
    return w, b


def kernel(x_prompt, x_sample, state_ssm_re, state_ssm_im, cache_band_k, cache_band_v, cache_mem_k, cache_mem_v, mem_prompt, norm_mix, norm_mem, norm_memin, norm_ffn, norm_kv, norm_final, ssm_a_re, ssm_a_im, ssm_log_dt, ssm_b_re, ssm_b_im, ssm_c_re, ssm_c_im, ssm_d, ssm_w_glu, w_kv_shared, attn_w_q, attn_rel_bias, attn_w_o, mem_w_q, mem_w_kv, mem_w_o, moe_w_group, moe_b_group, moe_w_expert, moe_b_expert, moe_w_in, moe_w_out):
    bp, seq, d = x_prompt.shape
    nb, t_dec, _ = x_sample.shape
    assert bp == 1
    n_prompt = bp * seq
    depth = norm_mix.shape[0]
    n_a = ssm_a_re.shape[0]
    dh = d // N_HEADS
    mt = mem_prompt.shape[1]
    x = jnp.concatenate([x_prompt.reshape(n_prompt, d), x_sample.reshape(nb * t_dec, d)], axis=0)
    mem = mem_prompt.reshape(mt, d)

    out_re_p, out_im_p, out_re_s, out_im_s, out_mk, out_mv = [], [], [], [], [], []
    def mixer_norms(layer):
        if layer < n_a:
            return [norm_mix[layer]], [F32]
        if layer == n_a:
            return [norm_kv, norm_mix[layer]], [BF16, BF16]
        return [norm_mix[layer]], [BF16]

    kv = None
    normed = rmsnorm(x, *mixer_norms(0))
    for layer in range(depth):
        if layer < n_a:
            a = layer
            (xn,) = normed
            prep = _ssm_prep(ssm_a_re[a], ssm_a_im[a], ssm_log_dt[a], ssm_b_re[a], ssm_b_im[a],
                             ssm_c_re[a], ssm_c_im[a])
            y, re_p, im_p, re_s, im_s = ssm_mix(xn, n_prompt, nb, state_ssm_re[a], state_ssm_im[a], prep)
            out_re_p.append(re_p)
            out_im_p.append(im_p)
            out_re_s.append(re_s)
            out_im_s.append(im_s)
            g = skip_gelu(y, xn, ssm_d[a])
            x = matmul(g, ssm_w_glu, prefix=(a,), n_out=d, glu=True, res=x, name="glu_proj")
        else:
            b = layer - n_a
            if layer == n_a:
                kvn, xn = normed
                kv = matmul(kvn, w_kv_shared, name="kv_proj")
            else:
                (xn,) = normed
            q = matmul(xn, attn_w_q, prefix=(b,), out_dtype=BF16, name="attn_q_proj")
            o = band_attention(q, kv, cache_band_k, cache_band_v, attn_rel_bias[b], n_prompt, nb)
            x = matmul(o, attn_w_o, prefix=(b,), res=x, name="attn_o_proj")

        (memn,) = rmsnorm(mem, [norm_memin[layer]], [BF16])
        kv_mem = matmul(memn, mem_w_kv, prefix=(layer,), name="mem_kv_proj")
        out_mk.append(kv_mem[:, :d].reshape(bp, mt, MEM_HEADS, d // MEM_HEADS))
        out_mv.append(kv_mem[:, d:].reshape(bp, mt, MEM_HEADS, d // MEM_HEADS))
        (xn,) = rmsnorm(x, [norm_mem[layer]], [BF16])
        q = matmul(xn, mem_w_q, prefix=(layer,), out_dtype=BF16, name="mem_q_proj")
        o = mem_attention(q, kv_mem, cache_mem_k, cache_mem_v, layer, n_prompt, nb)
        x = matmul(o, mem_w_o, prefix=(layer,), res=x, name="mem_o_proj")

        w_r, b_r = _router_weights(moe_w_group[layer], moe_b_group[layer], moe_w_expert[layer], moe_b_expert[layer])
        xn, logits = rmsnorm_router(x, norm_ffn[layer], w_r, b_r)
        if layer + 1 < depth:
            x, *normed = moe_ffn(x, xn, logits, moe_w_in, moe_w_out, layer, *mixer_norms(layer + 1))
        else:
            y_p, y_s = moe_ffn(x, xn, logits, moe_w_in, moe_w_out, layer, [norm_final], [F32], n_prompt)

    keep = min(PAST_CHUNKS * CHUNK, seq)
    k_all, v_all = kv[:, :d], kv[:, d:]
    bk_p = k_all[n_prompt - keep:n_prompt].reshape(bp, keep, N_HEADS, dh)
    bv_p = v_all[n_prompt - keep:n_prompt].reshape(bp, keep, N_HEADS, dh)
    keep_s = min(PAST_CHUNKS * CHUNK, t_dec)
    bk_s = k_all[n_prompt:].reshape(nb, t_dec, N_HEADS, dh)[:, t_dec - keep_s:]
    bv_s = v_all[n_prompt:].reshape(nb, t_dec, N_HEADS, dh)[:, t_dec - keep_s:]
    return (y_p.reshape(bp, seq, d), y_s.reshape(nb, t_dec, d),
            jnp.stack(out_re_p), jnp.stack(out_im_p), jnp.stack(out_re_s), jnp.stack(out_im_s),
            bk_p, bv_p, bk_s, bv_s, jnp.stack(out_mk), jnp.stack(out_mv))
```

```python
import functools
import math

import jax
import jax.numpy as jnp
from jax import lax
from jax.experimental import pallas as pl
from jax.experimental.pallas import tpu as pltpu

F32 = jnp.float32
BF16 = jnp.bfloat16

CHUNK = 64
RMS_EPS = 1e-6
SSM_GROUP = 16
SSM_STATE = 64
N_HEADS = 32
PAST_CHUNKS = 8
MAX_REL = 256
MEM_HEADS = 4
N_GROUPS = 8
EXPERTS_PER_GROUP = 8
N_EXPERTS = N_GROUPS * EXPERTS_PER_GROUP
TOP_K = 2

V7X_VMEM_BYTES = 64 * 1024 * 1024
LANES = 128
MXU_DIM = 256

BF16_ROWS = 16
ROW_TILE = 512
MATMUL_ROWS_MAX = 1152
NORM_TILE = 256
COL_TILE = 512
SSM_SUB = 8
SSM_TILE_GROUPS = LANES // SSM_GROUP
BAND_Q_ROWS = 1024
BAND_GROUP = 4
BAND_ROWS = BAND_GROUP * CHUNK
BAND_WIN = (PAST_CHUNKS + BAND_GROUP) * CHUNK
SAMPLE_HEADS_PER_STEP = 8
MOE_ROWS = 384
MOE_SUB = 128
MOE_COL = 256
COMBINE_ROWS = 128
GATHER_UNROLL = 8
GATHER_DMA_PRIORITY = 1
NEG = -1e30


def _params(sem, vmem_bytes):
    limit = min(int(vmem_bytes) + (4 << 20), V7X_VMEM_BYTES - (8 << 20))
    return pltpu.CompilerParams(dimension_semantics=sem, vmem_limit_bytes=limit)


def _rms_body(n_out, x_ref, *refs):
    g_refs, o_refs = refs[:n_out], refs[n_out:]
    x = x_ref[...]
    y = x * lax.rsqrt(jnp.mean(x * x, axis=-1, keepdims=True) + RMS_EPS)
    for g_ref, o_ref in zip(g_refs, o_refs):
        o_ref[...] = (y * g_ref[...]).astype(o_ref.dtype)


def rmsnorm(x, gains, dtypes):
    m, d = x.shape
    tm = min(NORM_TILE, m)
    n = len(gains)
    row = pl.BlockSpec((tm, d), lambda i: (i, 0))
    vec = pl.BlockSpec((1, d), lambda i: (0, 0))
    outs = pl.pallas_call(
        functools.partial(_rms_body, n),
        grid=(m // tm,),
        in_specs=[row] + [vec] * n,
        out_specs=[row] * n,
        out_shape=[jax.ShapeDtypeStruct((m, d), dt) for dt in dtypes],
        compiler_params=_params(("arbitrary",), tm * d * 4 * (2 * (1 + n) + 3)),
        name="rmsnorm",
    )(x, *[g.reshape(1, d).astype(F32) for g in gains])
    return outs


def _rms_router_body(x_ref, g_ref, wr_ref, br_ref, xn_ref, lg_ref):
    x = x_ref[...]
    y = x * lax.rsqrt(jnp.mean(x * x, axis=-1, keepdims=True) + RMS_EPS)
    xn = y * g_ref[...]
    xn_ref[...] = xn
    lg_ref[...] = jnp.dot(xn.astype(BF16), wr_ref[...], preferred_element_type=F32) + br_ref[...]


def rmsnorm_router(x, gain, w_router, b_router):
    m, d = x.shape
    tm = min(NORM_TILE, m)
    nr = w_router.shape[1]
    return pl.pallas_call(
        _rms_router_body,
        grid=(m // tm,),
        in_specs=[pl.BlockSpec((tm, d), lambda i: (i, 0)), pl.BlockSpec((1, d), lambda i: (0, 0)),
                  pl.BlockSpec((d, nr), lambda i: (0, 0)), pl.BlockSpec((1, nr), lambda i: (0, 0))],
        out_specs=[pl.BlockSpec((tm, d), lambda i: (i, 0)), pl.BlockSpec((tm, nr), lambda i: (i, 0))],
        out_shape=[jax.ShapeDtypeStruct((m, d), F32), jax.ShapeDtypeStruct((m, nr), F32)],
        compiler_params=_params(("arbitrary",), tm * d * (2 * 8 + 3 * 4) + 4 * d * nr),
        name="rmsnorm_router",
    )(x, gain.reshape(1, d).astype(F32), w_router, b_router)


def _mm_body(glu, has_res, x_ref, *refs):
    nw = 2 if glu else 1
    w_refs = refs[:nw]
    res_ref = refs[nw] if has_res else None
    o_ref = refs[nw + (1 if has_res else 0)]
    wb_refs = refs[nw + (1 if has_res else 0) + 1:]

    @pl.when(pl.program_id(1) == 0)
    def _():
        for w_ref, wb_ref in zip(w_refs, wb_refs):
            wb_ref[...] = w_ref[...].astype(BF16)

    x = x_ref[...]
    acc = jnp.dot(x, wb_refs[0][...], preferred_element_type=F32)
    if glu:
        gate = jnp.dot(x, wb_refs[1][...], preferred_element_type=F32)
        acc = acc * jax.nn.sigmoid(gate)
    if has_res:
        acc = res_ref[...] + acc
    o_ref[...] = acc.astype(o_ref.dtype)


def _matmul_rows(m):
    best = None
    for t in range(BF16_ROWS, min(m, MATMUL_ROWS_MAX) + 1, BF16_ROWS):
        if m % t == 0:
            best = t
    return best if best is not None else m


def matmul(x, w, prefix=(), n_out=None, glu=False, res=None, out_dtype=F32, name="matmul"):
    m, k = x.shape
    n = n_out if n_out is not None else w.shape[-1]
    tm = _matmul_rows(m)
    tn = min(COL_TILE // 2 if glu else COL_TILE, n)
    lead = (None,) * len(prefix)
    nblk = n // tn
    w_specs = [pl.BlockSpec(lead + (k, tn), lambda j, i: prefix + (0, j))]
    if glu:
        w_specs.append(pl.BlockSpec(lead + (k, tn), lambda j, i: prefix + (0, nblk + j)))
    nw = len(w_specs)
    in_specs = [pl.BlockSpec((tm, k), lambda j, i: (i, 0))] + w_specs
    args = [x] + [w] * nw
    if res is not None:
        in_specs.append(pl.BlockSpec((tm, tn), lambda j, i: (i, j)))
        args.append(res)
    osize = jnp.dtype(out_dtype).itemsize
    vmem = 2 * tm * k * 2 + nw * (2 * k * tn * 4 + k * tn * 2) + 2 * tm * tn * (osize + 4) + 4 * tm * tn * 4
    return pl.pallas_call(
        functools.partial(_mm_body, glu, res is not None),
        grid=(nblk, m // tm),
        in_specs=in_specs,
        out_specs=pl.BlockSpec((tm, tn), lambda j, i: (i, j)),
        out_shape=jax.ShapeDtypeStruct((m, n), out_dtype),
        scratch_shapes=[pltpu.VMEM((k, tn), BF16)] * nw,
        compiler_params=_params(("arbitrary", "arbitrary"), vmem),
        name=name,
    )(*args)


def _ssm_prep(a_re, a_im, log_dt, b_re, b_im, c_re, c_im):
    hi = lax.Precision.HIGHEST
    ls = SSM_SUB
    a_re = a_re.astype(F32)
    a_im = a_im.astype(F32)
    dt = jnp.exp(log_dt.astype(F32))[:, None]
    mag = jnp.exp(a_re * dt)
    ab_re = mag * jnp.cos(a_im * dt)
    ab_im = mag * jnp.sin(a_im * dt)
    den = a_re * a_re + a_im * a_im
    zr = ab_re - 1.0
    f_re = (zr * a_re + ab_im * a_im) / den
    f_im = (ab_im * a_re - zr * a_im) / den
    b_re = b_re.astype(F32)
    b_im = b_im.astype(F32)
    bb_re = f_re[..., None] * b_re - f_im[..., None] * b_im
    bb_im = f_re[..., None] * b_im + f_im[..., None] * b_re
    c_re = c_re.astype(F32)
    c_im = c_im.astype(F32)

    pw_re = [jnp.ones_like(ab_re)]
    pw_im = [jnp.zeros_like(ab_im)]
    for _ in range(ls):
        pr, pi = pw_re[-1], pw_im[-1]
        pw_re.append(pr * ab_re - pi * ab_im)
        pw_im.append(pr * ab_im + pi * ab_re)
    p_re = jnp.stack(pw_re, axis=1)
    p_im = jnp.stack(pw_im, axis=1)
    g = a_re.shape[0]

    cp_re = c_re[:, None] * p_re[:, :, None, :] - c_im[:, None] * p_im[:, :, None, :]
    cp_im = c_re[:, None] * p_im[:, :, None, :] + c_im[:, None] * p_re[:, :, None, :]
    kern = (jnp.einsum('gtpn,gnq->gtpq', cp_re[:, :ls], bb_re, precision=hi)
            - jnp.einsum('gtpn,gnq->gtpq', cp_im[:, :ls], bb_im, precision=hi))
    s_idx = jnp.arange(ls)[:, None]
    t_idx = jnp.arange(ls)[None, :]
    lag = t_idx - s_idx
    toep = kern[:, jnp.clip(lag, 0, ls - 1)]
    toep = jnp.where((lag >= 0)[None, :, :, None, None], toep, 0.0)
    toep = toep.transpose(0, 1, 4, 2, 3).reshape(g, ls * SSM_GROUP, ls * SSM_GROUP)

    pr_rev = p_re[:, ls - 1::-1][:, :, None, :]
    pi_rev = p_im[:, ls - 1::-1][:, :, None, :]
    bbt_re = bb_re.transpose(0, 2, 1)[:, None]
    bbt_im = bb_im.transpose(0, 2, 1)[:, None]
    win_re = pr_rev * bbt_re - pi_rev * bbt_im
    win_im = pr_rev * bbt_im + pi_rev * bbt_re
    win = jnp.concatenate([win_re, win_im], axis=-1).reshape(g, ls * SSM_GROUP, 2 * SSM_STATE)

    wo_re = cp_re[:, 1:].transpose(0, 3, 1, 2).reshape(g, SSM_STATE, ls * SSM_GROUP)
    wo_im = -cp_im[:, 1:].transpose(0, 3, 1, 2).reshape(g, SSM_STATE, ls * SSM_GROUP)
    wout = jnp.concatenate([wo_re, wo_im], axis=1)

    tg = SSM_TILE_GROUPS
    nt = g // tg
    lp = ls * SSM_GROUP
    c_t = toep.astype(BF16).reshape(nt, tg, ls, SSM_GROUP, lp).transpose(0, 2, 1, 3, 4).reshape(nt, ls * LANES, lp)
    c_win = win.astype(BF16).reshape(nt, tg, ls, SSM_GROUP, 2 * SSM_STATE).transpose(0, 2, 1, 3, 4).reshape(
        nt, ls * LANES, 2 * SSM_STATE)
    c_out = wout.astype(BF16).reshape(nt, tg, 2, SSM_STATE, lp).transpose(0, 2, 1, 3, 4).reshape(
        nt, 2 * tg * SSM_STATE, lp)
    al_re = p_re[:, ls].reshape(nt, tg * SSM_STATE)
    al_im = p_im[:, ls].reshape(nt, tg * SSM_STATE)
    a_rows = jnp.stack([jnp.concatenate([al_re, al_re], -1), jnp.concatenate([-al_im, al_im], -1)], axis=1)
    return c_t, c_win, c_out, a_rows


def _spread_groups(c, row_unit, col_unit):
    tg = SSM_TILE_GROUPS
    r, w = c.shape
    wide = w * tg
    src = lax.broadcasted_iota(jnp.int32, (w, wide), 0)
    dst = lax.broadcasted_iota(jnp.int32, (w, wide), 1)
    sel = ((src // col_unit == dst // (tg * col_unit)) & (src % col_unit == dst % col_unit)).astype(BF16)
    rep = jnp.dot(c, sel, preferred_element_type=F32)
    row_g = (lax.broadcasted_iota(jnp.int32, (r, wide), 0) // row_unit) % tg
    col_g = (lax.broadcasted_iota(jnp.int32, (r, wide), 1) // col_unit) % tg
    return jnp.where(row_g == col_g, rep, 0.0).astype(BF16)


def _ssm_body(n_prompt, nb, t_dec, xn_ref, ct_ref, cwin_ref, cout_ref, a_ref, h0_ref,
              y_ref, hp_ref, hs_ref, u_s, v_s, hin_s, pw_s, t_ref, win_ref, wout_ref):
    ls = SSM_SUB
    rp = n_prompt // ls
    ns = t_dec // ls
    half = v_s.shape[1] // 2
    t_ref[...] = _spread_groups(ct_ref[...], SSM_GROUP, SSM_GROUP)
    win_ref[...] = _spread_groups(cwin_ref[...], SSM_GROUP, SSM_STATE)
    wout_ref[...] = _spread_groups(cout_ref[...], SSM_STATE, SSM_GROUP)

    def sub_rows(ref, s, j):
        if j is None:
            return ref.at[pl.ds(s, rp, stride=ls), :]
        return ref.at[pl.ds(n_prompt + j * ls + s, nb, stride=t_dec), :]

    for s in range(ls):
        cols = slice(s * LANES, (s + 1) * LANES)
        u_s[0:rp, cols] = sub_rows(xn_ref, s, None)[...].astype(BF16)
        for j in range(ns):
            u_s[rp + j * nb:rp + (j + 1) * nb, cols] = sub_rows(xn_ref, s, j)[...].astype(BF16)

    v_s[...] = jnp.dot(u_s[...], win_ref[...], preferred_element_type=F32)
    a1 = a_ref[0:1, :]
    a2 = a_ref[1:2, :]

    def swap(h):
        return jnp.concatenate([h[:, half:], h[:, :half]], axis=1)

    def cmul(p, e):
        e1 = jnp.concatenate([e[:, :half], e[:, :half]], axis=1)
        e2 = jnp.concatenate([-e[:, half:], e[:, half:]], axis=1)
        return p * e1 + swap(p) * e2

    def step(h, v):
        return h * a1 + swap(h) * a2 + v

    hseg = rp // 2

    def body(c, carry):
        ha, hb, pw = carry
        hin_s[pl.ds(c, 1), :] = ha
        hin_s[pl.ds(hseg + c, 1), :] = hb
        pw_s[pl.ds(c, 1), :] = pw
        return (step(ha, v_s[pl.ds(c, 1), :]), step(hb, v_s[pl.ds(hseg + c, 1), :]),
                pw * a1 + swap(pw) * a2)

    zero = jnp.zeros((1, 2 * half), F32)
    one = jnp.concatenate([jnp.ones((1, half), F32), jnp.zeros((1, half), F32)], axis=1)
    ha, hb, pw = lax.fori_loop(0, hseg, body, (zero, zero, one), unroll=4)
    hin_s[hseg:rp, :] = hin_s[hseg:rp, :] + cmul(pw_s[...], ha)
    hp_ref[...] = hb + cmul(pw, ha)
    h = h0_ref[...]
    for j in range(ns):
        r0 = rp + j * nb
        hin_s[r0:r0 + nb, :] = h
        h = step(h, v_s[r0:r0 + nb, :])
    hs_ref[...] = h

    hin = hin_s[...].astype(BF16)
    for n0 in range(0, ls * LANES, MXU_DIM):
        k_hi = n0 + MXU_DIM
        ycol = (jnp.dot(u_s[:, :k_hi], t_ref[:k_hi, n0:n0 + MXU_DIM], preferred_element_type=F32)
                + jnp.dot(hin, wout_ref[:, n0:n0 + MXU_DIM], preferred_element_type=F32))
        for t in range(n0 // LANES, (n0 + MXU_DIM) // LANES):
            piece = ycol[:, t * LANES - n0:(t + 1) * LANES - n0]
            sub_rows(y_ref, t, None)[...] = piece[0:rp]
            for j in range(ns):
                sub_rows(y_ref, t, j)[...] = piece[rp + j * nb:rp + (j + 1) * nb]


def ssm_mix(xn, n_prompt, n_batch, h0_re, h0_im, prep):
    rows, d = xn.shape
    g = d // SSM_GROUP
    ls = SSM_SUB
    tg = SSM_TILE_GROUPS
    nt = g // tg
    t_dec = (rows - n_prompt) // n_batch
    r = n_prompt // ls + (t_dec // ls) * n_batch
    kw = ls * LANES
    sw = 2 * tg * SSM_STATE
    c_t, c_win, c_out, a_rows = prep
    h0 = jnp.concatenate([h0_re.astype(F32).reshape(n_batch, nt, sw // 2),
                          h0_im.astype(F32).reshape(n_batch, nt, sw // 2)], axis=-1).transpose(1, 0, 2)

    mat = lambda a, b: pl.BlockSpec((None, a, b), lambda i: (i, 0, 0))
    col = pl.BlockSpec((rows, LANES), lambda i: (0, i))
    vmem = (4 * rows * LANES * 4 + (kw * kw + 2 * kw * sw) * 2 + r * kw * 2 + 2 * r * sw * 4
            + r * sw * 4 + 2 * kw * max(kw, sw) * 4)
    y, hp, hs = pl.pallas_call(
        functools.partial(_ssm_body, n_prompt, n_batch, t_dec),
        grid=(nt,),
        in_specs=[col, mat(kw, c_t.shape[2]), mat(kw, c_win.shape[2]), mat(sw, c_out.shape[2]),
                  mat(2, sw), mat(n_batch, sw)],
        out_specs=[col, mat(1, sw), mat(n_batch, sw)],
        out_shape=[jax.ShapeDtypeStruct((rows, d), F32), jax.ShapeDtypeStruct((nt, 1, sw), F32),
                   jax.ShapeDtypeStruct((nt, n_batch, sw), F32)],
        scratch_shapes=[pltpu.VMEM((r, kw), BF16), pltpu.VMEM((r, sw), F32), pltpu.VMEM((r, sw), F32),
                        pltpu.VMEM((n_prompt // ls // 2, sw), F32), pltpu.VMEM((kw, kw), BF16), pltpu.VMEM((kw, sw), BF16), pltpu.VMEM((sw, kw), BF16)],
        compiler_params=_params(("arbitrary",), vmem),
        name="ssm",
    )(xn, c_t, c_win, c_out, a_rows, h0)

    half = sw // 2
    re_p = hp[:, 0, :half].reshape(g, SSM_STATE)[None]
    im_p = hp[:, 0, half:].reshape(g, SSM_STATE)[None]
    re_s = hs[:, :, :half].transpose(1, 0, 2).reshape(n_batch, g, SSM_STATE)
    im_s = hs[:, :, half:].transpose(1, 0, 2).reshape(n_batch, g, SSM_STATE)
    return y, re_p, im_p, re_s, im_s


def _gelu_body(y_ref, xn_ref, d_ref, o_ref):
    y = y_ref[...] + d_ref[...] * xn_ref[...]
    c = math.sqrt(2.0 / math.pi)
    cdf = 0.5 * (1.0 + jnp.tanh(c * (y + 0.044715 * (y * y * y))))
    o_ref[...] = (y * cdf).astype(o_ref.dtype)


def skip_gelu(y, xn, d_skip):
    m, d = y.shape
    tm = min(NORM_TILE, m)
    row = pl.BlockSpec((tm, d), lambda i: (i, 0))
    return pl.pallas_call(
        _gelu_body,
        grid=(m // tm,),
        in_specs=[row, row, pl.BlockSpec((1, d), lambda i: (0, 0))],
        out_specs=row,
        out_shape=jax.ShapeDtypeStruct((m, d), BF16),
        compiler_params=_params(("arbitrary",), tm * d * (2 * 10 + 4 * 4)),
        name="skip_gelu",
    )(y, xn, d_skip.reshape(1, d).astype(F32))


_NT = (((1,), (1,)), ((), ()))


def _softmax_rows(s):
    m = jnp.max(s, axis=-1, keepdims=True)
    e = jnp.exp(s - m)
    return e / jnp.sum(e, axis=-1, keepdims=True)


def _memattn_body(q_ref, k_ref, v_ref, o_ref, kb, vb):
    @pl.when(pl.program_id(0) == 0)
    def _():
        kb[...] = k_ref[...].astype(BF16)
        vb[...] = v_ref[...].astype(BF16)

    dh = q_ref.shape[1] // MEM_HEADS
    scale = dh ** -0.5
    for h in range(MEM_HEADS):
        sl = slice(h * dh, (h + 1) * dh)
        s = lax.dot_general(q_ref[:, sl], kb[:, sl], _NT, preferred_element_type=F32) * scale
        p = _softmax_rows(s).astype(BF16)
        o_ref[:, sl] = jnp.dot(p, vb[:, sl], preferred_element_type=F32).astype(o_ref.dtype)


def _memattn_cache_body(q_ref, k_ref, v_ref, prev_ref, o_ref):
    del prev_ref
    n_heads, dh = k_ref.shape[1], k_ref.shape[2]
    scale = dh ** -0.5
    t = q_ref.shape[0]
    heads = [slice(h * dh, (h + 1) * dh) for h in range(n_heads)]
    q = q_ref[...].astype(F32)
    s = jnp.concatenate([lax.dot_general(q[:, sl], k_ref[:, h, :], _NT, preferred_element_type=F32)
                         for h, sl in enumerate(heads)], axis=0)
    p = _softmax_rows(s * scale)
    for h, sl in enumerate(heads):
        o_ref[:, sl] = jnp.dot(p[h * t:(h + 1) * t], v_ref[:, h, :],
                               preferred_element_type=F32).astype(o_ref.dtype)


def mem_attention(q, kv_prompt, cache_k, cache_v, layer, n_prompt, n_batch):
    rows, d = q.shape
    mt = kv_prompt.shape[0]
    tm = min(ROW_TILE, n_prompt)
    t_dec = (rows - n_prompt) // n_batch
    scratch = [pltpu.VMEM((mt, d), BF16), pltpu.VMEM((mt, d), BF16)]
    vmem_kv = 2 * 2 * mt * d * 4 + 2 * mt * d * 2
    o = pl.pallas_call(
        _memattn_body,
        grid=(n_prompt // tm,),
        in_specs=[pl.BlockSpec((tm, d), lambda i: (i, 0)),
                  pl.BlockSpec((mt, d), lambda i: (0, 0)),
                  pl.BlockSpec((mt, d), lambda i: (0, 1))],
        out_specs=pl.BlockSpec((tm, d), lambda i: (i, 0)),
        out_shape=jax.ShapeDtypeStruct((rows, d), BF16),
        scratch_shapes=scratch,
        compiler_params=_params(("arbitrary",), vmem_kv + 4 * tm * d * 2 + 2 * tm * d * 4),
        name="mem_attn_prompt",
    )(q, kv_prompt, kv_prompt)
    pb = n_prompt // t_dec
    dh = d // MEM_HEADS
    slab = pl.BlockSpec((None, None, mt, MEM_HEADS, dh), lambda b: (layer, b, 0, 0, 0))
    return pl.pallas_call(
        _memattn_cache_body,
        grid=(n_batch,),
        in_specs=[pl.BlockSpec((t_dec, d), lambda b: (pb + b, 0)), slab, slab,
                  pl.BlockSpec(memory_space=pl.ANY)],
        out_specs=pl.BlockSpec((t_dec, d), lambda b: (pb + b, 0)),
        out_shape=jax.ShapeDtypeStruct((rows, d), BF16),
        input_output_aliases={3: 0},
        compiler_params=_params(("arbitrary",), 2 * 2 * 2 * mt * d * 4 + 8 * t_dec * d * 4),
        name="mem_attn_sample",
    )(q, cache_k, cache_v, o)


def _band_prompt_body(n_groups, q_ref, k_ref, v_ref, b_ref, o_ref, kb, vb):
    qb = pl.program_id(1)
    s_len, dh = k_ref.shape
    past = PAST_CHUNKS * CHUNK

    @pl.when(qb == 0)
    def _():
        for ref, src in ((kb, k_ref), (vb, v_ref)):
            ref[0:past, :] = jnp.zeros((past, dh), BF16)
            ref[past:past + s_len, :] = src[...].astype(BF16)

    bias = b_ref[...]
    col = lax.broadcasted_iota(jnp.int32, (BAND_ROWS, BAND_WIN), 1)
    scale = dh ** -0.5

    def group(gi, carry):
        c0 = (qb * n_groups + gi) * BAND_GROUP
        r0 = pl.multiple_of(gi * BAND_ROWS, BAND_ROWS)
        w0 = pl.multiple_of(c0 * CHUNK, BAND_ROWS)
        q = q_ref[pl.ds(r0, BAND_ROWS), :]
        s = lax.dot_general(q, kb[pl.ds(w0, BAND_WIN), :], _NT, preferred_element_type=F32) * scale + bias
        s = jnp.where(col >= (PAST_CHUNKS - c0) * CHUNK, s, NEG)
        p = _softmax_rows(s).astype(BF16)
        o_ref[pl.ds(r0, BAND_ROWS), :] = jnp.dot(p, vb[pl.ds(w0, BAND_WIN), :],
                                                 preferred_element_type=F32).astype(o_ref.dtype)
        return carry

    lax.fori_loop(0, n_groups, group, 0, unroll=True)


def _band_sample_body(hb, dh, q_ref, kc_ref, vc_ref, kn_ref, vn_ref, bc_ref, bn_ref, prev_ref, o_ref):
    del prev_ref
    scale = dh ** -0.5
    t = q_ref.shape[0]
    heads = [slice(h * dh, (h + 1) * dh) for h in range(hb)]
    q = q_ref[...].astype(F32)
    s1 = jnp.concatenate([lax.dot_general(q[:, sl], kc_ref[:, h, :], _NT,
                                          preferred_element_type=F32) * scale + bc_ref[h]
                          for h, sl in enumerate(heads)], axis=0)
    s2 = jnp.concatenate([lax.dot_general(q[:, sl], kn_ref[:, sl], _NT,
                                          preferred_element_type=F32) * scale + bn_ref[h]
                          for h, sl in enumerate(heads)], axis=0)
    m = jnp.maximum(jnp.max(s1, axis=-1, keepdims=True), jnp.max(s2, axis=-1, keepdims=True))
    e1 = jnp.exp(s1 - m)
    e2 = jnp.exp(s2 - m)
    den = jnp.sum(e1, axis=-1, keepdims=True) + jnp.sum(e2, axis=-1, keepdims=True)
    p1 = e1 / den
    p2 = e2 / den
    for h, sl in enumerate(heads):
        rows = slice(h * t, (h + 1) * t)
        o = (jnp.dot(p1[rows], vc_ref[:, h, :], preferred_element_type=F32)
             + jnp.dot(p2[rows], vn_ref[:, sl], preferred_element_type=F32))
        o_ref[:, sl] = o.astype(o_ref.dtype)


def _rel_bias(table, offset, n_q, n_k):
    p = n_q + n_k
    k = jnp.arange(p)
    diff = jnp.where(k < n_k, k, k - p)
    w = table[:, jnp.clip(offset - diff, -MAX_REL, MAX_REL) + MAX_REL].astype(F32)
    h = table.shape[0]
    flat = jnp.broadcast_to(w[:, None, :], (h, n_q, p)).reshape(h, n_q * p)
    return flat[:, :n_q * (p - 1)].reshape(h, n_q, p - 1)[:, :, :n_k]


def band_attention(q, kv, cache_k, cache_v, table, n_prompt, n_batch):
    rows, d = q.shape
    nh = N_HEADS
    dh = d // nh
    past = PAST_CHUNKS * CHUNK
    band = past + CHUNK
    t_dec = (rows - n_prompt) // n_batch
    cb = cache_k.shape[1]

    lo = (jnp.arange(BAND_ROWS)[:, None] // CHUNK) * CHUNK
    col = jnp.arange(BAND_WIN)[None, :]
    in_band = (col >= lo) & (col < lo + band)
    bias_p = jnp.where(in_band[None], _rel_bias(table, past, BAND_ROWS, BAND_WIN), NEG)
    bias_s = _rel_bias(table, cb, t_dec, cb + t_dec)
    bias_c, bias_n = bias_s[:, :, :cb], bias_s[:, :, cb:]

    bq = min(BAND_Q_ROWS, n_prompt)
    pad_rows = past + n_prompt
    o = pl.pallas_call(
        functools.partial(_band_prompt_body, bq // BAND_ROWS),
        grid=(nh, n_prompt // bq),
        in_specs=[pl.BlockSpec((bq, dh), lambda h, i: (i, h)),
                  pl.BlockSpec((n_prompt, dh), lambda h, i: (0, h)),
                  pl.BlockSpec((n_prompt, dh), lambda h, i: (0, nh + h)),
                  pl.BlockSpec((None, BAND_ROWS, BAND_WIN), lambda h, i: (h, 0, 0))],
        out_specs=pl.BlockSpec((bq, dh), lambda h, i: (i, h)),
        out_shape=jax.ShapeDtypeStruct((rows, d), BF16),
        scratch_shapes=[pltpu.VMEM((pad_rows, dh), BF16), pltpu.VMEM((pad_rows, dh), BF16)],
        compiler_params=_params(("arbitrary", "arbitrary"),
                                4 * n_prompt * dh * 4 + 2 * pad_rows * dh * 2 + 8 * bq * dh * 2),
        name="band_attn_prompt",
    )(q, kv, kv, bias_p)

    hb = min(SAMPLE_HEADS_PER_STEP, nh)
    ngrp = nh // hb
    pb = n_prompt // t_dec
    w = hb * dh
    return pl.pallas_call(
        functools.partial(_band_sample_body, hb, dh),
        grid=(n_batch, ngrp),
        in_specs=[pl.BlockSpec((t_dec, w), lambda b, g: (pb + b, g)),
                  pl.BlockSpec((None, cb, hb, dh), lambda b, g: (b, 0, g, 0)),
                  pl.BlockSpec((None, cb, hb, dh), lambda b, g: (b, 0, g, 0)),
                  pl.BlockSpec((t_dec, w), lambda b, g: (pb + b, g)),
                  pl.BlockSpec((t_dec, w), lambda b, g: (pb + b, ngrp + g)),
                  pl.BlockSpec((hb, t_dec, cb), lambda b, g: (g, 0, 0)),
                  pl.BlockSpec((hb, t_dec, t_dec), lambda b, g: (g, 0, 0)),
                  pl.BlockSpec(memory_space=pl.ANY)],
        out_specs=pl.BlockSpec((t_dec, w), lambda b, g: (pb + b, g)),
        out_shape=jax.ShapeDtypeStruct((rows, d), BF16),
        input_output_aliases={7: 0},
        compiler_params=_params(("arbitrary", "arbitrary"), 4 * cb * w * 4 + 16 * t_dec * w * 4 + 4 * hb * t_dec * cb * 4),
        name="band_attn_sample",
    )(q, cache_k, cache_v, kv, kv, bias_c, bias_n, o)


def _row_copy(src_hbm, dst_ref, sem, r, src_row):
    return pltpu.make_async_copy(src_hbm.at[pl.ds(src_row, 1), :], dst_ref.at[pl.ds(r, 1), :], sem)


def _issue_rows(src_hbm, idx_ref, dst_ref, n_rows, sem):
    def body(b, carry):
        for k in range(GATHER_UNROLL):
            r = b * GATHER_UNROLL + k
            _row_copy(src_hbm, dst_ref, sem, r, idx_ref[0, r]).start(priority=GATHER_DMA_PRIORITY)
        return carry
    lax.fori_loop(0, n_rows // GATHER_UNROLL, body, 0)


def _drain_rows(src_hbm, dst_ref, n_rows, sem):
    def body(b, carry):
        for k in range(GATHER_UNROLL):
            _row_copy(src_hbm, dst_ref, sem, b * GATHER_UNROLL + k, 0).wait()
        return carry
    lax.fori_loop(0, n_rows // GATHER_UNROLL, body, 0)


def _moe_body(item_e, item_blk, item_nsub, slot_cur, slot_next, xn_hbm, wg_ref, wu_ref, wo_ref, o_ref,
              xbuf, sem):
    del item_e, item_blk
    i = pl.program_id(0)
    j = pl.program_id(1)
    n_items = pl.num_programs(0)
    nsub = item_nsub[i]
    slot = i % 2

    @pl.when((nsub > 0) & (j == 0))
    def _():
        @pl.when(i == 0)
        def _():
            _issue_rows(xn_hbm, slot_cur, xbuf.at[0], nsub * MOE_SUB, sem.at[0])

        _drain_rows(xn_hbm, xbuf.at[slot], nsub * MOE_SUB, sem.at[slot])
        nsub_next = item_nsub[jnp.minimum(i + 1, n_items - 1)]

        @pl.when((i + 1 < n_items) & (nsub_next > 0))
        def _():
            _issue_rows(xn_hbm, slot_next, xbuf.at[1 - slot], nsub_next * MOE_SUB, sem.at[1 - slot])

        o_ref[...] = jnp.zeros_like(o_ref)

    @pl.when(nsub > 0)
    def _():
        xcur = xbuf.at[slot]

        def sub(s, carry):
            r0 = pl.multiple_of(s * MOE_SUB, MOE_SUB)
            xb = xcur[pl.ds(r0, MOE_SUB), :]
            g = jnp.dot(xb, wg_ref[...], preferred_element_type=F32)
            u = jnp.dot(xb, wu_ref[...], preferred_element_type=F32)
            h = g * jax.nn.sigmoid(g) * u
            o_ref[pl.ds(r0, MOE_SUB), :] += jnp.dot(h, wo_ref[...], preferred_element_type=F32)
            return carry

        lax.fori_loop(0, nsub, sub, 0)


def moe_experts(xn, slot_tok, w_in, w_out, layer, item_e, item_blk, item_nsub):
    d = xn.shape[1]
    n_items = slot_tok.shape[0]
    n_slots = n_items * MOE_ROWS
    de = w_out.shape[2]
    tj = min(MOE_COL, de)
    nj = de // tj

    def jj(i, j, ie, ib, ns):
        return jnp.where(ns[i] > 0, j, nj - 1)

    grid_spec = pltpu.PrefetchScalarGridSpec(
        num_scalar_prefetch=3,
        grid=(n_items, nj),
        in_specs=[pl.BlockSpec((None, 1, MOE_ROWS), lambda i, j, ie, ib, ns: (ib[i], 0, 0),
                               memory_space=pltpu.SMEM),
                  pl.BlockSpec((None, 1, MOE_ROWS),
                               lambda i, j, ie, ib, ns: (ib[jnp.minimum(i + 1, n_items - 1)], 0, 0),
                               memory_space=pltpu.SMEM),
                  pl.BlockSpec(memory_space=pl.ANY),
                  pl.BlockSpec((None, None, d, tj), lambda i, j, ie, ib, ns: (layer, ie[i], 0, jj(i, j, ie, ib, ns))),
                  pl.BlockSpec((None, None, d, tj),
                               lambda i, j, ie, ib, ns: (layer, ie[i], 0, nj + jj(i, j, ie, ib, ns))),
                  pl.BlockSpec((None, None, tj, d), lambda i, j, ie, ib, ns: (layer, ie[i], jj(i, j, ie, ib, ns), 0))],
        out_specs=pl.BlockSpec((MOE_ROWS, d), lambda i, j, ie, ib, ns: (ib[i], 0)),
        scratch_shapes=[pltpu.VMEM((2, MOE_ROWS, d), F32), pltpu.SemaphoreType.DMA((2,))],
    )
    vmem = 4 * MOE_ROWS * d * 4 + 3 * d * tj * 2 * 4 + 4 * MOE_SUB * d * 4
    return pl.pallas_call(
        _moe_body,
        grid_spec=grid_spec,
        out_shape=jax.ShapeDtypeStruct((n_slots, d), F32),
        compiler_params=_params(("arbitrary", "arbitrary"), vmem),
        name="moe_experts",
    )(item_e, item_blk, item_nsub, slot_tok, slot_tok, xn, w_in, w_in, w_out)


def _route(logits):
    t = logits.shape[0]
    lg = logits[:, :N_GROUPS]
    g_idx = jnp.argmax(lg, axis=-1).astype(jnp.int32)[:, None]
    in_group = lax.broadcasted_iota(jnp.int32, (t, N_GROUPS), 1) == g_idx
    p_group = jnp.sum(jnp.where(in_group, jax.nn.softmax(lg, axis=-1), 0.0), axis=-1, keepdims=True)
    le = logits[:, N_GROUPS:N_GROUPS + N_EXPERTS].reshape(t, N_GROUPS, EXPERTS_PER_GROUP)
    le = jnp.sum(jnp.where(in_group[:, :, None], le, 0.0), axis=1)
    lane = lax.broadcasted_iota(jnp.int32, (t, EXPERTS_PER_GROUP), 1)
    tops_v, tops_j = [], []
    for _ in range(TOP_K):
        j = jnp.argmax(le, axis=-1).astype(jnp.int32)[:, None]
        tops_j.append(j)
        tops_v.append(jnp.max(le, axis=-1, keepdims=True))
        le = jnp.where(lane == j, -jnp.inf, le)
    top_v = jnp.concatenate(tops_v, axis=-1)
    top_j = jnp.concatenate(tops_j, axis=-1)
    gate = p_group * jax.nn.softmax(top_v, axis=-1)
    eidx = g_idx * EXPERTS_PER_GROUP + top_j
    return eidx.astype(jnp.int32), gate


def _dispatch(eidx):
    t = eidx.shape[0]
    a = t * TOP_K
    n_items = a // MOE_ROWS + N_EXPERTS
    flat_e = eidx.reshape(a)
    order = jnp.argsort(flat_e).astype(jnp.int32)
    se = flat_e[order]
    expert_ids = jnp.arange(N_EXPERTS, dtype=jnp.int32)
    counts = jnp.sum((flat_e[:, None] == expert_ids[None, :]).astype(jnp.int32), axis=0)
    nblk = (counts + MOE_ROWS - 1) // MOE_ROWS
    blk_end = jnp.cumsum(nblk)
    blk_start = blk_end - nblk
    start = jnp.cumsum(counts) - counts
    shift = blk_start * MOE_ROWS - start
    dest_sorted = jnp.arange(a, dtype=jnp.int32) + jnp.sum(
        jnp.where(se[:, None] == expert_ids[None, :], shift[None, :], 0), axis=1)
    slot_tok = jnp.zeros((n_items * MOE_ROWS,), jnp.int32).at[dest_sorted].set(order // TOP_K)
    dest = jnp.zeros((a,), jnp.int32).at[order].set(dest_sorted).reshape(t, TOP_K)
    n_used = blk_end[-1]
    item = jnp.arange(n_items, dtype=jnp.int32)
    item_c = jnp.minimum(item, n_used - 1)
    item_e = jnp.minimum(jnp.searchsorted(blk_end, item_c, side='right'), N_EXPERTS - 1).astype(jnp.int32)
    rows_left = counts[item_e] - (item_c - blk_start[item_e]) * MOE_ROWS
    nsub = (jnp.clip(rows_left, 0, MOE_ROWS) + MOE_SUB - 1) // MOE_SUB
    item_nsub = jnp.where(item < n_used, nsub, 0).astype(jnp.int32)
    return slot_tok.reshape(n_items, 1, MOE_ROWS), dest, item_e, item_c.astype(jnp.int32), item_nsub


def _combine_body(n_blocks, n_gain, split_blocks, d_cur, d_next, gate_ref, x_ref, *refs):
    g_refs = refs[:n_gain]
    ys_hbm = refs[n_gain]
    o_refs = refs[n_gain + 1:-2]
    buf, sem = refs[-2:]
    i = pl.program_id(0)
    slot = i % 2
    tb = x_ref.shape[0]

    rows_per_iter = GATHER_UNROLL // TOP_K

    def row_copy(s, k, r, src_row):
        return _row_copy(ys_hbm, buf.at[s, k], sem.at[s], r, src_row)

    def issue(idx_ref, s):
        def body(b, carry):
            for u in range(rows_per_iter):
                r = b * rows_per_iter + u
                for k in range(TOP_K):
                    row_copy(s, k, r, idx_ref[k, r]).start(priority=GATHER_DMA_PRIORITY)
            return carry
        lax.fori_loop(0, tb // rows_per_iter, body, 0)

    @pl.when(i == 0)
    def _():
        issue(d_cur, 0)

    @pl.when(i + 1 < n_blocks)
    def _():
        issue(d_next, 1 - slot)

    def drain(b, carry):
        for u in range(rows_per_iter):
            for k in range(TOP_K):
                row_copy(slot, k, b * rows_per_iter + u, 0).wait()
        return carry

    lax.fori_loop(0, tb // rows_per_iter, drain, 0)
    acc = buf[slot, 0] * gate_ref[:, 0:1]
    for k in range(1, TOP_K):
        acc = acc + buf[slot, k] * gate_ref[:, k:k + 1]
    x = x_ref[...] + acc
    y = x * lax.rsqrt(jnp.mean(x * x, axis=-1, keepdims=True) + RMS_EPS)
    if split_blocks is None:
        o_refs[0][...] = x
        for g_ref, o_ref in zip(g_refs, o_refs[1:]):
            o_ref[...] = (y * g_ref[...]).astype(o_ref.dtype)
    else:
        y = y * g_refs[0][...]

        @pl.when(i < split_blocks)
        def _():
            o_refs[0][...] = y

        @pl.when(i >= split_blocks)
        def _():
            o_refs[1][...] = y


def moe_combine(x, ys, dest, gate, gains, dtypes, n_prompt=None):
    t, d = x.shape
    tb = min(COMBINE_ROWS, t) if n_prompt is None else math.gcd(COMBINE_ROWS, math.gcd(n_prompt, t - n_prompt))
    n_blocks = t // tb
    n_gain = len(gains)
    dest3 = dest.reshape(n_blocks, tb, TOP_K).transpose(0, 2, 1)
    idx = lambda f: pl.BlockSpec((None, TOP_K, tb), lambda i: (f(i), 0, 0), memory_space=pltpu.SMEM)
    row = pl.BlockSpec((tb, d), lambda i: (i, 0))
    vec = pl.BlockSpec((1, d), lambda i: (0, 0))
    if n_prompt is None:
        split = None
        out_specs = [row] * (1 + n_gain)
        out_shape = [jax.ShapeDtypeStruct((t, d), F32)] + [jax.ShapeDtypeStruct((t, d), dt) for dt in dtypes]
    else:
        split = n_prompt // tb
        out_specs = [pl.BlockSpec((tb, d), lambda i: (jnp.minimum(i, split - 1), 0)),
                     pl.BlockSpec((tb, d), lambda i: (jnp.maximum(i - split, 0), 0))]
        out_shape = [jax.ShapeDtypeStruct((n_prompt, d), dtypes[0]),
                     jax.ShapeDtypeStruct((t - n_prompt, d), dtypes[0])]
    return pl.pallas_call(
        functools.partial(_combine_body, n_blocks, n_gain, split),
        grid=(n_blocks,),
        in_specs=[idx(lambda i: i), idx(lambda i: jnp.minimum(i + 1, n_blocks - 1)),
                  pl.BlockSpec((tb, TOP_K), lambda i: (i, 0)), row] + [vec] * n_gain
                 + [pl.BlockSpec(memory_space=pl.ANY)],
        out_specs=out_specs,
        out_shape=out_shape,
        scratch_shapes=[pltpu.VMEM((2, TOP_K, tb, d), F32), pltpu.SemaphoreType.DMA((2,))],
        compiler_params=_params(("arbitrary",), (2 * TOP_K + 8 + 2 * len(out_shape)) * tb * d * 4),
        name="moe_combine",
    )(dest3, dest3, gate, x, *[g.reshape(1, d).astype(F32) for g in gains], ys)


def moe_ffn(x, xn, logits, w_in, w_out, layer, gains, dtypes, n_prompt=None):
    eidx, gate = _route(logits)
    slot_tok, dest, item_e, item_blk, item_nsub = _dispatch(eidx)
    ys = moe_experts(xn, slot_tok, w_in, w_out, layer, item_e, item_blk, item_nsub)
    return moe_combine(x, ys, dest, gate, gains, dtypes, n_prompt)


def _router_weights(w_group, b_group, w_expert, b_expert):
    d = w_group.shape[0]
    pad = LANES - N_GROUPS - N_EXPERTS
    w = jnp.concatenate([w_group, w_expert, jnp.zeros((d, pad), w_group.dtype)], axis=1).astype(BF16)
    b = jnp.concatenate([b_group.astype(F32), b_expert.astype(F32), jnp.zeros((pad,), F32)]).reshape(1, LANES)
    return w, b


def kernel(x_prompt, x_sample, state_ssm_re, state_ssm_im, cache_band_k, cache_band_v, cache_mem_k, cache_mem_v, mem_prompt, norm_mix, norm_mem, norm_memin, norm_ffn, norm_kv, norm_final, ssm_a_re, ssm_a_im, ssm_log_dt, ssm_b_re, ssm_b_im, ssm_c_re, ssm_c_im, ssm_d, ssm_w_glu, w_kv_shared, attn_w_q, attn_rel_bias, attn_w_o, mem_w_q, mem_w_kv, mem_w_o, moe_w_group, moe_b_group, moe_w_expert, moe_b_expert, moe_w_in, moe_w_out):
    bp, seq, d = x_prompt.shape
    nb, t_dec, _ = x_sample.shape
    assert bp == 1
    n_prompt = bp * seq
    depth = norm_mix.shape[0]
    n_a = ssm_a_re.shape[0]
    dh = d // N_HEADS
    mt = mem_prompt.shape[1]
    x = jnp.concatenate([x_prompt.reshape(n_prompt, d), x_sample.reshape(nb * t_dec, d)], axis=0)
    mem = mem_prompt.reshape(mt, d)

    out_re_p, out_im_p, out_re_s, out_im_s, out_mk, out_mv = [], [], [], [], [], []
    def mixer_norms(layer):
        if layer < n_a:
            return [norm_mix[layer]], [F32]
        if layer == n_a:
            return [norm_kv, norm_mix[layer]], [BF16, BF16]
        return [norm_mix[layer]], [BF16]

    kv = None
    normed = rmsnorm(x, *mixer_norms(0))
    for layer in range(depth):
        if layer < n_a:
            a = layer
            (xn,) = normed
            prep = _ssm_prep(ssm_a_re[a], ssm_a_im[a], ssm_log_dt[a], ssm_b_re[a], ssm_b_im[a],
                             ssm_c_re[a], ssm_c_im[a])
            y, re_p, im_p, re_s, im_s = ssm_mix(xn, n_prompt, nb, state_ssm_re[a], state_ssm_im[a], prep)
            out_re_p.append(re_p)
            out_im_p.append(im_p)
            out_re_s.append(re_s)
            out_im_s.append(im_s)
            g = skip_gelu(y, xn, ssm_d[a])
            x = matmul(g, ssm_w_glu, prefix=(a,), n_out=d, glu=True, res=x, name="glu_proj")
        else:
            b = layer - n_a
            if layer == n_a:
                kvn, xn = normed
                kv = matmul(kvn, w_kv_shared, name="kv_proj")
            else:
                (xn,) = normed
            q = matmul(xn, attn_w_q, prefix=(b,), out_dtype=BF16, name="attn_q_proj")
            o = band_attention(q, kv, cache_band_k, cache_band_v, attn_rel_bias[b], n_prompt, nb)
            x = matmul(o, attn_w_o, prefix=(b,), res=x, name="attn_o_proj")

        (memn,) = rmsnorm(mem, [norm_memin[layer]], [BF16])
        kv_mem = matmul(memn, mem_w_kv, prefix=(layer,), name="mem_kv_proj")
        out_mk.append(kv_mem[:, :d].reshape(bp, mt, MEM_HEADS, d // MEM_HEADS))
        out_mv.append(kv_mem[:, d:].reshape(bp, mt, MEM_HEADS, d // MEM_HEADS))
        (xn,) = rmsnorm(x, [norm_mem[layer]], [BF16])
        q = matmul(xn, mem_w_q, prefix=(layer,), out_dtype=BF16, name="mem_q_proj")
        o = mem_attention(q, kv_mem, cache_mem_k, cache_mem_v, layer, n_prompt, nb)
        x = matmul(o, mem_w_o, prefix=(layer,), res=x, name="mem_o_proj")

        w_r, b_r = _router_weights(moe_w_group[layer], moe_b_group[layer], moe_w_expert[layer], moe_b_expert[layer])
        xn, logits = rmsnorm_router(x, norm_ffn[layer], w_r, b_r)
        if layer + 1 < depth:
            x, *normed = moe_ffn(x, xn, logits, moe_w_in, moe_w_out, layer, *mixer_norms(layer + 1))
        else:
            y_p, y_s = moe_ffn(x, xn, logits, moe_w_in, moe_w_out, layer, [norm_final], [F32], n_prompt)

    keep = min(PAST_CHUNKS * CHUNK, seq)
    k_all, v_all = kv[:, :d], kv[:, d:]
    bk_p = k_all[n_prompt - keep:n_prompt].reshape(bp, keep, N_HEADS, dh)
    bv_p = v_all[n_prompt - keep:n_prompt].reshape(bp, keep, N_HEADS, dh)
    keep_s = min(PAST_CHUNKS * CHUNK, t_dec)
    bk_s = k_all[n_prompt:].reshape(nb, t_dec, N_HEADS, dh)[:, t_dec - keep_s:]
    bv_s = v_all[n_prompt:].reshape(nb, t_dec, N_HEADS, dh)[:, t_dec - keep_s:]
    return (y_p.reshape(bp, seq, d), y_s.reshape(nb, t_dec, d),
            jnp.stack(out_re_p), jnp.stack(out_im_p), jnp.stack(out_re_s), jnp.stack(out_im_s),
            bk_p, bv_p, bk_s, bv_s, jnp.stack(out_mk), jnp.stack(out_mv))
```
